```python
import jax, jax.numpy as jnp
from jax import lax
import numpy as np

D_MODEL = 2048
BATCH = 1
SEQ = 16384
DEPTH = 1

CHUNK = 64
EPS = 1e-6
HG_HEADS = 8
HG_KDIM = 128
HG_VDIM = 128
HG_KW = HG_HEADS * HG_KDIM
HG_VW = HG_HEADS * HG_VDIM
AT_HEADS = 8
AT_HDIM = 128
AT_W = AT_HEADS * AT_HDIM
BAND_CHUNKS = 9
REL_CLIP = 128
MIX_WIDTH = HG_VW + AT_W
IN_SPLITS = (HG_KW, HG_KW, HG_VW, HG_VW, AT_W, AT_W, AT_W)
IN_COLS = sum(IN_SPLITS)
D_FF = 5632
CONV_W = 3

kernel_name = "hybrid_hgrn2_chunkattn_convffn"


def rms_norm(x, gain):
    xf = x.astype(jnp.float32)
    y = xf * lax.rsqrt(jnp.mean(xf * xf, axis=-1, keepdims=True) + EPS)
    return (y * gain.astype(jnp.float32)).astype(x.dtype)


def hgrn2_chunk_scan(q, k, v, log_f):
    B, S, H, K = q.shape
    V = v.shape[-1]
    N = S // CHUNK

    def to_chunks(t):
        return t.reshape(B, N, CHUNK, H, t.shape[-1]).transpose(1, 0, 3, 2, 4)

    tri = jnp.tril(jnp.ones((CHUNK, CHUNK), dtype=bool))

    def step(state, xs):
        q_c, k_c, v_c, g_c = xs
        b = jnp.cumsum(g_c, axis=-2)
        inter = jnp.einsum('bhck,bhkv->bhcv', q_c * jnp.exp(b), state)
        diff = b[:, :, :, None, :] - b[:, :, None, :, :]
        decay = jnp.exp(jnp.where(tri[None, None, :, :, None], diff, -jnp.inf))
        attn = jnp.einsum('bhtk,bhtsk,bhsk->bhts', q_c, decay, k_c)
        intra = jnp.einsum('bhts,bhsv->bhtv', attn, v_c)
        b_last = b[:, :, -1, :]
        new_state = jnp.exp(b_last)[..., None] * state + jnp.einsum(
            'bhsk,bhsv->bhkv', k_c * jnp.exp(b_last[:, :, None, :] - b), v_c)
        return new_state, inter + intra

    state0 = jnp.zeros((B, H, K, V), jnp.float32)
    _, o = lax.scan(step, state0, (to_chunks(q), to_chunks(k), to_chunks(v), to_chunks(log_f)))
    return o.transpose(1, 0, 3, 2, 4).reshape(B, S, H, V)


def chunk_band_attention(q, k, v, rel_table):
    B, S, H, D = q.shape
    N = S // CHUNK
    pad = BAND_CHUNKS - 1
    L = BAND_CHUNKS * CHUNK
    qc = q.reshape(B, N, CHUNK, H, D)
    kp = jnp.pad(k.reshape(B, N, CHUNK, H, D), ((0, 0), (pad, 0), (0, 0), (0, 0), (0, 0)))
    vp = jnp.pad(v.reshape(B, N, CHUNK, H, D), ((0, 0), (pad, 0), (0, 0), (0, 0), (0, 0)))
    band_idx = jnp.arange(N)[:, None] + jnp.arange(BAND_CHUNKS)[None, :]
    kb = kp[:, band_idx].reshape(B, N, L, H, D)
    vb = vp[:, band_idx].reshape(B, N, L, H, D)
    scores = jnp.einsum('bnqhd,bnkhd->bhnqk', qc, kb).astype(jnp.float32) * (D ** -0.5)
    t = jnp.arange(CHUNK)
    u = jnp.arange(L)
    dist = pad * CHUNK + t[:, None] - u[None, :]
    ridx = jnp.clip(dist, -REL_CLIP, REL_CLIP) + REL_CLIP
    bias = rel_table.astype(jnp.float32)[:, ridx]
    valid = (jnp.arange(N)[:, None] - pad + jnp.arange(BAND_CHUNKS)[None, :]) >= 0
    valid = jnp.repeat(valid, CHUNK, axis=1)
    scores = jnp.where(valid[None, None, :, None, :], scores + bias[None, :, None], -jnp.inf)
    p = jax.nn.softmax(scores, axis=-1).astype(v.dtype)
    o = jnp.einsum('bhnqk,bnkhd->bnqhd', p, vb)
    return o.reshape(B, S, H * D)


def causal_dwconv(x, w, b):
    y = lax.conv_general_dilated(
        x, w[:, None, :], window_strides=(1,), padding=[(CONV_W - 1, 0)],
        dimension_numbers=('NWC', 'WIO', 'NWC'), feature_group_count=x.shape[-1])
    return y + b


def setup_inputs(seed: int = 0) -> dict:
    key = jax.random.key(seed)
    ks = jax.random.split(key, 16)
    f32 = jnp.float32
    nrm = lambda k, shape, s: jax.random.normal(k, shape, f32) * s
    return {
        "x": nrm(ks[0], (BATCH, SEQ, D_MODEL), 1.0),
        "g_mix": 1.0 + nrm(ks[1], (DEPTH, D_MODEL), 0.02),
        "w_in": nrm(ks[2], (DEPTH, D_MODEL, IN_COLS), D_MODEL ** -0.5),
        "hg_lb": nrm(ks[3], (DEPTH + 1, HG_KW), 1.0),
        "hg_out_gain": 1.0 + nrm(ks[4], (DEPTH, HG_VW), 0.02),
        "q_gain": 1.0 + nrm(ks[5], (DEPTH, AT_HDIM), 0.02),
        "k_gain": 1.0 + nrm(ks[6], (DEPTH, AT_HDIM), 0.02),
        "rel_bias": nrm(ks[7], (DEPTH, AT_HEADS, 2 * REL_CLIP + 1), 0.1),
        "at_out_gain": 1.0 + nrm(ks[8], (DEPTH, AT_W), 0.02),
        "w_out": nrm(ks[9], (DEPTH, MIX_WIDTH, D_MODEL), MIX_WIDTH ** -0.5),
        "g_ffn": 1.0 + nrm(ks[10], (DEPTH, D_MODEL), 0.02),
        "w_up": nrm(ks[11], (DEPTH, D_MODEL, 2 * D_FF), D_MODEL ** -0.5),
        "conv_w": nrm(ks[12], (DEPTH, CONV_W, 2 * D_FF), CONV_W ** -0.5),
        "conv_b": nrm(ks[13], (DEPTH, 2 * D_FF), 0.02),
        "w_down": nrm(ks[14], (DEPTH, D_FF, D_MODEL), D_FF ** -0.5),
    }


def reference(x, g_mix, w_in, hg_lb, hg_out_gain, q_gain, k_gain, rel_bias,
              at_out_gain, w_out, g_ffn, w_up, conv_w, conv_b, w_down):
    B, S, _ = x.shape
    f32 = jnp.float32
    lb_all = jnp.cumsum(jax.nn.softmax(hg_lb.astype(f32), axis=0), axis=0)
    cuts = list(np.cumsum(IN_SPLITS)[:-1])
    for l in range(DEPTH):
        h = rms_norm(x, g_mix[l])
        proj = h @ w_in[l]
        hq, hf, hi, hg, aq, ak, av = jnp.split(proj, cuts, axis=-1)

        lb = lb_all[l]
        log_f = jnp.logaddexp(jnp.log(lb), jnp.log1p(-lb) + jax.nn.log_sigmoid(hf.astype(f32)))
        k_in = -jnp.expm1(log_f)
        q_hg = jax.nn.silu(hq.astype(f32))
        shp_k = (B, S, HG_HEADS, HG_KDIM)
        o_hg = hgrn2_chunk_scan(q_hg.reshape(shp_k), k_in.reshape(shp_k),
                                hi.astype(f32).reshape(B, S, HG_HEADS, HG_VDIM),
                                log_f.reshape(shp_k))
        o_hg = o_hg * lax.rsqrt(jnp.mean(o_hg * o_hg, axis=-1, keepdims=True) + EPS)
        o_hg = (o_hg.reshape(B, S, HG_VW) * hg_out_gain[l].astype(f32)
                * jax.nn.silu(hg.astype(f32))).astype(x.dtype)

        shp_a = (B, S, AT_HEADS, AT_HDIM)
        qa = rms_norm(aq.reshape(shp_a), q_gain[l])
        ka = rms_norm(ak.reshape(shp_a), k_gain[l])
        o_at = chunk_band_attention(qa, ka, av.reshape(shp_a), rel_bias[l])
        o_at = rms_norm(o_at, at_out_gain[l])

        x = x + jnp.concatenate([o_hg, o_at], axis=-1) @ w_out[l]

        h2 = rms_norm(x, g_ffn[l])
        up = causal_dwconv(h2 @ w_up[l], conv_w[l], conv_b[l])
        a, gate = jnp.split(up, 2, axis=-1)
        x = x + (a * jax.nn.silu(gate)) @ w_down[l]
    return x
```

```python
import functools

import jax
import jax.numpy as jnp
import numpy as np
from jax import lax
from jax.experimental import pallas as pl
from jax.experimental.pallas import tpu as pltpu

F32 = jnp.float32
BF16 = jnp.bfloat16

CHUNK = 64
EPS = 1e-6
HEADS = 8
HDIM = 128
GROUP_W = HEADS * HDIM
N_GROUPS = 7
BAND_CHUNKS = 9
REL_CLIP = 128
CONV_W = 3

COL_Q_HG, COL_K_HG, COL_V_HG, COL_GATE, COL_Q_AT, COL_K_AT, COL_V_AT = (
    0, 8, 16, 24, 32, 40, 48)

ATT_BLOCK = 512
ATT_SUB = 128
ATT_KEYS = ATT_SUB + (BAND_CHUNKS - 1) * CHUNK
BIAS_EXT = 768

LEVELS = (32, 16, 8, 4, 2, 1)

VMEM_LIMIT = 56 * 1024 * 1024


def _dot(a, b):
    return jnp.dot(a, b, preferred_element_type=F32)


def _dot_nt(a, b):
    return lax.dot_general(a, b, (((1,), (1,)), ((), ())), preferred_element_type=F32)


def _dot_tn(a, b):
    return lax.dot_general(a, b, (((0,), (0,)), ((), ())), preferred_element_type=F32)


def _sigmoid(x):
    return 1.0 / (1.0 + jnp.exp(-x))


def _rms(x):
    return x * lax.rsqrt(jnp.mean(x * x, axis=-1, keepdims=True) + EPS)


def _inproj_kernel(x_ref, gmix_ref, w_ref, lbp_ref, qg_ref, kg_ref, out_ref, g_ref):
    h = (_rms(x_ref[...]) * gmix_ref[...]).astype(BF16)

    def proj(j):
        return _dot(h, w_ref[:, j * GROUP_W:(j + 1) * GROUP_W])

    def put(j, val):
        out_ref[:, j * GROUP_W:(j + 1) * GROUP_W] = val.astype(BF16)

    a = proj(0)
    put(0, a * _sigmoid(a))

    a = proj(1)
    p0 = lbp_ref[0:1, :]
    p1 = lbp_ref[1:2, :]
    pm = jnp.maximum(p0, p1)
    e0 = jnp.exp(p0 - pm)
    e1 = jnp.exp(p1 - pm)
    lb = e0 / (e0 + e1)
    s = _sigmoid(a)
    g_ref[...] = jnp.log(lb + (1.0 - lb) * s)
    put(1, (1.0 - lb) * (1.0 - s))

    put(2, proj(2))

    a = proj(3)
    put(3, a * _sigmoid(a))

    scale = HDIM ** -0.5
    for j, gain_ref, mul in ((4, qg_ref, scale), (5, kg_ref, 1.0)):
        a = proj(j)
        gain = gain_ref[...] * mul
        for hd in range(HEADS):
            ah = a[:, hd * HDIM:(hd + 1) * HDIM]
            c0 = j * GROUP_W + hd * HDIM
            out_ref[:, c0:c0 + HDIM] = (_rms(ah) * gain).astype(BF16)

    put(6, proj(6))


def _inproj(x2, g_mix, w_in, hg_lb, q_gain, k_gain, tm):
    S, D = x2.shape
    n_out = w_in.shape[1]
    full = lambda shape: pl.BlockSpec(shape, lambda i: (0, 0))
    return pl.pallas_call(
        _inproj_kernel,
        grid=(S // tm,),
        in_specs=[
            pl.BlockSpec((tm, D), lambda i: (i, 0)),
            full((1, D)),
            pl.BlockSpec((D, n_out), lambda i: (0, 0), pipeline_mode=pl.Buffered(1)),
            full(hg_lb.shape),
            full((1, HDIM)),
            full((1, HDIM)),
        ],
        out_specs=[
            pl.BlockSpec((tm, n_out), lambda i: (i, 0)),
            pl.BlockSpec((tm, GROUP_W), lambda i: (i, 0)),
        ],
        out_shape=[
            jax.ShapeDtypeStruct((S, n_out), BF16),
            jax.ShapeDtypeStruct((S, GROUP_W), F32),
        ],
        compiler_params=pltpu.CompilerParams(
            dimension_semantics=("arbitrary",), vmem_limit_bytes=VMEM_LIMIT),
        name="inproj",
    )(x2, g_mix, w_in, hg_lb, q_gain, k_gain)


def _hgrn_sum_matrix():
    t = np.arange(CHUNK)[:, None]
    s = np.arange(CHUNK)[None, :]
    mats = [s <= t, s > t]
    for m in LEVELS:
        mid = (t // (2 * m)) * (2 * m) + m
        in_b = (t % (2 * m)) >= m
        mats.append(np.where(in_b, (s >= mid) & (s <= t), (s > t) & (s < mid)))
    return np.concatenate(mats, axis=0).astype(np.float32)


def _hgrn_kernel(msum_ref, q_ref, k_ref, v_ref, g_ref, gate_ref, gain_ref, o_ref, state_ref,
                 *, n_chunks):
    @pl.when(pl.program_id(1) == 0)
    def _():
        state_ref[...] = jnp.zeros_like(state_ref)

    row = lax.broadcasted_iota(jnp.int32, (CHUNK, HDIM), 0)
    tt = lax.broadcasted_iota(jnp.int32, (CHUNK, CHUNK), 0)
    ss = lax.broadcasted_iota(jnp.int32, (CHUNK, CHUNK), 1)
    in_b = {m: (row & m) != 0 for m in LEVELS}
    keep = {m: (tt // (2 * m)) == (ss // (2 * m)) for m in LEVELS}
    diag = tt == ss

    msum = msum_ref[...]
    gain = gain_ref[...]
    state = state_ref[...]
    for c in range(n_chunks):
        rows = slice(c * CHUNK, (c + 1) * CHUNK)
        q = q_ref[rows, :]
        k = k_ref[rows, :]
        v = v_ref[rows, :]
        g = g_ref[rows, :]
        g_hi = g.astype(BF16)
        g_lo = (g - g_hi.astype(F32)).astype(BF16)
        e2 = _dot(msum, jnp.concatenate([g_hi, g_lo], axis=1))
        dec = jnp.exp(e2[:, :HDIM] + e2[:, HDIM:])
        qf = q.astype(F32)
        kf = k.astype(F32)

        attn = jnp.where(diag, _dot_nt(q, k), 0.0)
        for li, m in enumerate(LEVELS):
            d = dec[(2 + li) * CHUNK:(3 + li) * CHUNK]
            qm = jnp.where(in_b[m], qf * d, 0.0).astype(BF16)
            km = jnp.where(in_b[m], 0.0, kf * d).astype(BF16)
            attn = attn + jnp.where(keep[m], _dot_nt(qm, km), 0.0)

        q_in = (qf * dec[0:CHUNK]).astype(BF16)
        k_out = (kf * dec[CHUNK:2 * CHUNK]).astype(BF16)
        o = _dot_nt(q_in, state.astype(BF16)) + _dot(attn.astype(BF16), v)
        state = state * dec[CHUNK - 1:CHUNK, :] + _dot_tn(v, k_out)

        o = _rms(o) * gain * gate_ref[rows, :].astype(F32)
        o_ref[rows, :] = o.astype(o_ref.dtype)
    state_ref[...] = state


def _hgrn(proj, g, out_gain, tc):
    S = proj.shape[0]
    msum = jnp.asarray(_hgrn_sum_matrix(), BF16)
    col = lambda off: pl.BlockSpec((tc, HDIM), lambda h, i: (i, off + h))
    return pl.pallas_call(
        functools.partial(_hgrn_kernel, n_chunks=tc // CHUNK),
        grid=(HEADS, S // tc),
        in_specs=[
            pl.BlockSpec(msum.shape, lambda h, i: (0, 0)),
            col(COL_Q_HG), col(COL_K_HG), col(COL_V_HG),
            pl.BlockSpec((tc, HDIM), lambda h, i: (i, h)),
            col(COL_GATE),
            pl.BlockSpec((1, HDIM), lambda h, i: (0, h)),
        ],
        out_specs=pl.BlockSpec((tc, HDIM), lambda h, i: (i, h)),
        out_shape=jax.ShapeDtypeStruct((S, GROUP_W), BF16),
        scratch_shapes=[pltpu.VMEM((HDIM, HDIM), F32)],
        compiler_params=pltpu.CompilerParams(
            dimension_semantics=("arbitrary", "arbitrary"), vmem_limit_bytes=VMEM_LIMIT),
        name="hgrn2",
    )(msum, proj, proj, proj, g, proj, out_gain)


def _bias_kernel(row_ref, o_ref):
    x = jnp.broadcast_to(row_ref[0], (ATT_SUB, BIAS_EXT))
    y = pltpu.roll(x, 0, 1, stride=1, stride_axis=0)[:, :ATT_KEYS]
    r = lax.broadcasted_iota(jnp.int32, (ATT_SUB, ATT_KEYS), 0)
    u = lax.broadcasted_iota(jnp.int32, (ATT_SUB, ATT_KEYS), 1)
    lo = jnp.where(r < CHUNK, 0, CHUNK)
    visible = (u >= lo) & (u < lo + BAND_CHUNKS * CHUNK)
    o_ref[0] = jnp.where(visible, y, -jnp.inf)


def _rel_bias(rel_bias):
    far = rel_bias[:, 2 * REL_CLIP:]
    pad = BAND_CHUNKS * CHUNK - REL_CLIP - CHUNK
    ext = jnp.concatenate([
        jnp.broadcast_to(far, (HEADS, pad)),
        rel_bias[:, :0:-1],
        jnp.broadcast_to(far, (HEADS, BIAS_EXT - pad - 2 * REL_CLIP)),
    ], axis=1).reshape(HEADS, 1, BIAS_EXT)
    return pl.pallas_call(
        _bias_kernel,
        grid=(HEADS,),
        in_specs=[pl.BlockSpec((1, 1, BIAS_EXT), lambda h: (h, 0, 0))],
        out_specs=pl.BlockSpec((1, ATT_SUB, ATT_KEYS), lambda h: (h, 0, 0)),
        out_shape=jax.ShapeDtypeStruct((HEADS, ATT_SUB, ATT_KEYS), F32),
        name="rel_bias",
    )(ext)


def _attn_kernel(bias_ref, q_ref, kp_ref, kc_ref, vp_ref, vc_ref, o_ref):
    first = pl.program_id(1) == 0
    kcat = jnp.concatenate([kp_ref[...], kc_ref[...]], axis=0)
    vcat = jnp.concatenate([vp_ref[...], vc_ref[...]], axis=0)
    bias = bias_ref[0]
    u = lax.broadcasted_iota(jnp.int32, (ATT_SUB, ATT_KEYS), 1)
    for j in range(ATT_BLOCK // ATT_SUB):
        rows = slice(j * ATT_SUB, (j + 1) * ATT_SUB)
        keys = slice(j * ATT_SUB, j * ATT_SUB + ATT_KEYS)
        s = _dot_nt(q_ref[rows, :], kcat[keys]) + bias
        n_before = jnp.where(first, ATT_BLOCK - j * ATT_SUB, 0)
        s = jnp.where(u < n_before, -jnp.inf, s)
        p = jnp.exp(s - jnp.max(s, axis=-1, keepdims=True))
        o = _dot(p.astype(BF16), vcat[keys]) / jnp.sum(p, axis=-1, keepdims=True)
        o_ref[rows, :] = o.astype(o_ref.dtype)


def _attention(proj, bias):
    S = proj.shape[0]
    cur = lambda off: pl.BlockSpec((ATT_BLOCK, HDIM), lambda h, i: (i, off + h))
    prev = lambda off: pl.BlockSpec(
        (ATT_BLOCK, HDIM), lambda h, i: (jnp.maximum(i - 1, 0), off + h))
    return pl.pallas_call(
        _attn_kernel,
        grid=(HEADS, S // ATT_BLOCK),
        in_specs=[
            pl.BlockSpec((1, ATT_SUB, ATT_KEYS), lambda h, i: (h, 0, 0)),
            cur(COL_Q_AT), prev(COL_K_AT), cur(COL_K_AT), prev(COL_V_AT), cur(COL_V_AT),
        ],
        out_specs=pl.BlockSpec((ATT_BLOCK, HDIM), lambda h, i: (i, h)),
        out_shape=jax.ShapeDtypeStruct((S, GROUP_W), BF16),
        compiler_params=pltpu.CompilerParams(
            dimension_semantics=("arbitrary", "arbitrary"), vmem_limit_bytes=VMEM_LIMIT),
        name="band_attn",
    )(bias, proj, proj, proj, proj, proj)


def _outproj_kernel(ohg_ref, oat_ref, x_ref, w_ref, again_ref, gffn_ref, x1_ref, h2_ref):
    oat = (_rms(oat_ref[...].astype(F32)) * again_ref[...]).astype(BF16)
    x1 = x_ref[...] + _dot(ohg_ref[...], w_ref[0:GROUP_W, :]) + _dot(oat, w_ref[GROUP_W:, :])
    x1_ref[...] = x1
    h2_ref[...] = (_rms(x1) * gffn_ref[...]).astype(BF16)


def _outproj(o_hg, o_at, x2, w_out, at_gain, g_ffn, tm):
    S, D = x2.shape
    full = lambda shape: pl.BlockSpec(shape, lambda i: (0, 0))
    rows = lambda w: pl.BlockSpec((tm, w), lambda i: (i, 0))
    return pl.pallas_call(
        _outproj_kernel,
        grid=(S // tm,),
        in_specs=[rows(GROUP_W), rows(GROUP_W), rows(D),
                  pl.BlockSpec(w_out.shape, lambda i: (0, 0), pipeline_mode=pl.Buffered(1)),
                  full((1, GROUP_W)), full((1, D))],
        out_specs=[rows(D), rows(D)],
        out_shape=[jax.ShapeDtypeStruct((S, D), F32), jax.ShapeDtypeStruct((S, D), BF16)],
        compiler_params=pltpu.CompilerParams(
            dimension_semantics=("arbitrary",), vmem_limit_bytes=VMEM_LIMIT),
        name="outproj",
    )(o_hg, o_at, x2, w_out, at_gain, g_ffn)


def _conv3(u, cw, cb):
    return (cw[2:3, :] * u + cw[1:2, :] * pltpu.roll(u, 1, 0)
            + cw[0:1, :] * pltpu.roll(u, 2, 0) + cb)


def _ffn_up_kernel(h_ref, wa_ref, wg_ref, cwa_ref, cwg_ref, cba_ref, cbg_ref, o_ref,
                   carry_a, carry_g):
    @pl.when(pl.program_id(1) == 0)
    def _():
        carry_a[...] = jnp.zeros_like(carry_a)
        carry_g[...] = jnp.zeros_like(carry_g)

    h = h_ref[...]
    tm = h.shape[0]
    ua = _dot(h, wa_ref[...])
    ug = _dot(h, wg_ref[...])

    def gated(ya, yg):
        return (ya * (yg * _sigmoid(yg))).astype(o_ref.dtype)

    o_ref[...] = gated(_conv3(ua, cwa_ref[...], cba_ref[...]),
                       _conv3(ug, cwg_ref[...], cbg_ref[...]))
    ta = jnp.concatenate([carry_a[...], ua[0:8]], axis=0)
    tg = jnp.concatenate([carry_g[...], ug[0:8]], axis=0)
    o_ref[0:8, :] = gated(_conv3(ta, cwa_ref[...], cba_ref[...])[8:16],
                          _conv3(tg, cwg_ref[...], cbg_ref[...])[8:16])
    carry_a[...] = ua[tm - 8:tm]
    carry_g[...] = ug[tm - 8:tm]


def _ffn_up(h2, w_up, conv_w, conv_b, tm, tn):
    S, D = h2.shape
    d_ff = w_up.shape[1] // 2
    nc = d_ff // tn
    a_col = lambda r: pl.BlockSpec((r, tn), lambda c, m: (0, c))
    g_col = lambda r: pl.BlockSpec((r, tn), lambda c, m: (0, nc + c))
    return pl.pallas_call(
        _ffn_up_kernel,
        grid=(nc, S // tm),
        in_specs=[pl.BlockSpec((tm, D), lambda c, m: (m, 0)),
                  a_col(D), g_col(D), a_col(CONV_W), g_col(CONV_W), a_col(1), g_col(1)],
        out_specs=pl.BlockSpec((tm, tn), lambda c, m: (m, c)),
        out_shape=jax.ShapeDtypeStruct((S, d_ff), BF16),
        scratch_shapes=[pltpu.VMEM((8, tn), F32), pltpu.VMEM((8, tn), F32)],
        compiler_params=pltpu.CompilerParams(
            dimension_semantics=("arbitrary", "arbitrary"), vmem_limit_bytes=VMEM_LIMIT),
        name="ffn_up",
    )(h2, w_up, w_up, conv_w, conv_w, conv_b, conv_b)


def _ffn_down_kernel(a_ref, w_ref, x1_ref, o_ref):
    o_ref[...] = x1_ref[...] + _dot(a_ref[...], w_ref[...])


def _ffn_down(act, w_down, x1, tm):
    S, D = x1.shape
    d_ff = act.shape[1]
    return pl.pallas_call(
        _ffn_down_kernel,
        grid=(S // tm,),
        in_specs=[pl.BlockSpec((tm, d_ff), lambda i: (i, 0)),
                  pl.BlockSpec((d_ff, D), lambda i: (0, 0), pipeline_mode=pl.Buffered(1)),
                  pl.BlockSpec((tm, D), lambda i: (i, 0))],
        out_specs=pl.BlockSpec((tm, D), lambda i: (i, 0)),
        out_shape=jax.ShapeDtypeStruct((S, D), F32),
        compiler_params=pltpu.CompilerParams(
            dimension_semantics=("arbitrary",), vmem_limit_bytes=VMEM_LIMIT),
        name="ffn_down",
    )(act, w_down, x1)


def kernel(x, g_mix, w_in, hg_lb, hg_out_gain, q_gain, k_gain, rel_bias, at_out_gain, w_out,
           g_ffn, w_up, conv_w, conv_b, w_down):
    B, S, D = x.shape
    depth = g_mix.shape[0]
    assert B == 1 and depth == 1 and S % ATT_BLOCK == 0
    assert w_in.shape[2] == N_GROUPS * GROUP_W
    x2 = x.reshape(S, D)
    proj, g = _inproj(x2, g_mix, w_in[0].astype(BF16), hg_lb, q_gain, k_gain, tm=256)
    o_hg = _hgrn(proj, g, hg_out_gain, tc=256)
    o_at = _attention(proj, _rel_bias(rel_bias[0]))
    x1, h2 = _outproj(o_hg, o_at, x2, w_out[0].astype(BF16), at_out_gain, g_ffn, tm=512)
    act = _ffn_up(h2, w_up[0].astype(BF16), conv_w[0], conv_b, tm=512, tn=512)
    out = _ffn_down(act, w_down[0].astype(BF16), x1, tm=512)
    return out.reshape(B, S, D)
```

```python
import functools

import jax
import jax.numpy as jnp
import numpy as np
from jax import lax
from jax.experimental import pallas as pl
from jax.experimental.pallas import tpu as pltpu

F32 = jnp.float32
BF16 = jnp.bfloat16

CHUNK = 64
EPS = 1e-6
HEADS = 8
HDIM = 128
GROUP_W = HEADS * HDIM
N_GROUPS = 7
BAND_CHUNKS = 9
REL_CLIP = 128
CONV_W = 3
SUBLANES = 8

COL_Q_HG, COL_K_HG, COL_V_HG, COL_GATE, COL_Q_AT, COL_K_AT, COL_V_AT = (
    0, 8, 16, 24, 32, 40, 48)

ATT_BLOCK = 512
ATT_SUB = 128
ATT_KEYS = ATT_SUB + (BAND_CHUNKS - 1) * CHUNK
BIAS_EXT = 768

VMEM_LIMIT = 56 * 1024 * 1024


def _dot(a, b):
    return jnp.dot(a, b, preferred_element_type=F32)


def _dot_nt(a, b):
    return lax.dot_general(a, b, (((1,), (1,)), ((), ())), preferred_element_type=F32)


def _dot_tn(a, b):
    return lax.dot_general(a, b, (((0,), (0,)), ((), ())), preferred_element_type=F32)


def _sigmoid(x):
    return 1.0 / (1.0 + jnp.exp(-x))


def _tril(n):
    return jnp.asarray(np.tril(np.ones((n, n), np.float32)), BF16)


def _rms(x):
    return x * lax.rsqrt(jnp.mean(x * x, axis=-1, keepdims=True) + EPS)


def _inproj_kernel(x_ref, gmix_ref, w_ref, lbp_ref, qg_ref, kg_ref, tri_ref, out_ref, b_ref):
    h = (_rms(x_ref[...]) * gmix_ref[...]).astype(BF16)

    def proj(j):
        return _dot(h, w_ref[:, j * GROUP_W:(j + 1) * GROUP_W])

    def put(j, val):
        out_ref[:, j * GROUP_W:(j + 1) * GROUP_W] = val.astype(BF16)

    a = proj(0)
    put(0, a * _sigmoid(a))

    a = proj(1)
    p0 = lbp_ref[0:1, :]
    p1 = lbp_ref[1:2, :]
    pm = jnp.maximum(p0, p1)
    e0 = jnp.exp(p0 - pm)
    e1 = jnp.exp(p1 - pm)
    lb = e0 / (e0 + e1)
    s = _sigmoid(a)
    put(1, (1.0 - lb) * (1.0 - s))
    g = jnp.log2(lb + (1.0 - lb) * s)
    g_hi = g.astype(BF16)
    g_lo = (g - g_hi.astype(F32)).astype(BF16)
    b_ref[...] = _dot(tri_ref[...], g_hi) + _dot(tri_ref[...], g_lo)

    put(2, proj(2))

    a = proj(3)
    put(3, a * _sigmoid(a))

    scale = HDIM ** -0.5
    for j, gain_ref, mul in ((4, qg_ref, scale), (5, kg_ref, 1.0)):
        a = proj(j)
        gain = gain_ref[...] * mul
        for hd in range(HEADS):
            ah = a[:, hd * HDIM:(hd + 1) * HDIM]
            c0 = j * GROUP_W + hd * HDIM
            out_ref[:, c0:c0 + HDIM] = (_rms(ah) * gain).astype(BF16)

    put(6, proj(6))


def _inproj(x2, g_mix, w_in, hg_lb, q_gain, k_gain, tm):
    S, D = x2.shape
    n_out = w_in.shape[1]
    full = lambda shape: pl.BlockSpec(shape, lambda i: (0, 0))
    return pl.pallas_call(
        _inproj_kernel,
        grid=(S // tm,),
        in_specs=[
            pl.BlockSpec((tm, D), lambda i: (i, 0)),
            full((1, D)),
            pl.BlockSpec((D, n_out), lambda i: (0, 0), pipeline_mode=pl.Buffered(1)),
            full(hg_lb.shape),
            full((1, HDIM)),
            full((1, HDIM)),
            full((tm, tm)),
        ],
        out_specs=[
            pl.BlockSpec((tm, n_out), lambda i: (i, 0)),
            pl.BlockSpec((tm, GROUP_W), lambda i: (i, 0)),
        ],
        out_shape=[
            jax.ShapeDtypeStruct((S, n_out), BF16),
            jax.ShapeDtypeStruct((S, GROUP_W), F32),
        ],
        compiler_params=pltpu.CompilerParams(
            dimension_semantics=("arbitrary",), vmem_limit_bytes=VMEM_LIMIT),
        name="inproj",
    )(x2, g_mix, w_in, hg_lb, q_gain, k_gain, _tril(tm))


HG_BLOCK = 256
HG_HEADS_PER_STEP = 2
HG_HIGH = (128, 64, 32, 16, 8)
HG_LOW = (4, 2, 1)


def _hgrn_kernel(q_ref, k_ref, v_ref, b_ref, gate_ref, gain_ref, o_ref, state_ref):
    T = HG_BLOCK
    H = T // 2
    NH = HG_HEADS_PER_STEP

    @pl.when(pl.program_id(1) == 0)
    def _():
        state_ref[...] = jnp.zeros_like(state_ref)

    row = lax.broadcasted_iota(jnp.int32, (T, HDIM), 0)
    tt = lax.broadcasted_iota(jnp.int32, (T, T), 0)
    ss = lax.broadcasted_iota(jnp.int32, (T, T), 1)
    hs = lax.broadcasted_iota(jnp.int32, (H, T), 1)
    diag = tt == ss
    same2, same4, same8 = ((tt // n) == (ss // n) for n in (2, 4, 8))
    heads = range(NH)
    col = lambda hd: slice(hd * HDIM, (hd + 1) * HDIM)
    q = [q_ref[:, col(hd)] for hd in heads]
    k = [k_ref[:, col(hd)] for hd in heads]
    v = [v_ref[:, col(hd)] for hd in heads]
    b = [b_ref[:, col(hd)] for hd in heads]

    inter = []
    for hd in heads:
        state = state_ref[hd]
        b_last = b[hd][T - 1:T, :]
        inter.append(_dot_nt(q[hd] * jnp.exp2(b[hd]).astype(BF16), state.astype(BF16)))
        k_out = k[hd] * jnp.exp2(b_last - b[hd]).astype(BF16)
        state_ref[hd] = state * jnp.exp2(b_last) + _dot_tn(v[hd], k_out)

    attn = []
    for hd in heads:
        bh = b[hd]
        prev1 = bh - pltpu.roll(bh, 1, 0)
        prev2 = bh - pltpu.roll(bh, 2, 0)
        next1 = pltpu.roll(bh, T - 1, 0) - bh
        b3 = bh.reshape(T // SUBLANES, SUBLANES, HDIM)
        mid = jnp.broadcast_to(b3[:, 3:4, :], b3.shape).reshape(T, HDIM)
        pos = row & 3
        exps = {
            4: jnp.where((row & 4) != 0, bh - mid, mid - bh),
            2: jnp.where(pos == 2, prev1, jnp.where(pos == 3, prev2,
                                                   jnp.where(pos == 0, next1, 0.0))),
            1: jnp.where((row & 1) != 0, prev1, 0.0),
        }
        prods = {}
        for m in HG_LOW:
            in_b = (row & m) != 0
            dec = jnp.exp2(exps[m])
            q_m = q[hd] * jnp.where(in_b, dec, 0.0).astype(BF16)
            k_m = k[hd] * jnp.where(in_b, 0.0, dec).astype(BF16)
            prods[m] = _dot_nt(q_m, k_m)
        pd = _dot_nt(q[hd], k[hd])
        attn.append(jnp.where(same2, jnp.where(diag, pd, prods[1]),
                              jnp.where(same4, prods[2], jnp.where(same8, prods[4], 0.0))))

    for m in HG_HIGH:
        nb = T // (2 * m)
        split = lambda a: a.reshape(nb, 2 * m, HDIM)
        second = lambda a: split(a)[:, m:, :].reshape(H, HDIM)
        keep = ((lax.broadcasted_iota(jnp.int32, (H, T), 0) // m) == (hs // (2 * m))) & (
            (hs & m) == 0)
        for hd in heads:
            b3 = split(b[hd])
            mid = b3[:, m - 1:m, :]
            dec_b = jnp.exp2(b3[:, m:, :] - mid).reshape(H, HDIM)
            dec_a = jnp.concatenate(
                [jnp.exp2(mid - b3[:, :m, :]), jnp.zeros((nb, m, HDIM), F32)], axis=1
            ).reshape(T, HDIM)
            if m >= 2 * SUBLANES:
                q_b = second(q[hd]) * dec_b.astype(BF16)
            else:
                q_b = (second(q[hd].astype(F32)) * dec_b).astype(BF16)
            k_a = k[hd] * dec_a.astype(BF16)
            p = _dot_nt(q_b, k_a)
            a3 = attn[hd].reshape(nb, 2 * m, T)
            upd = jnp.where(keep, p, a3[:, m:, :].reshape(H, T)).reshape(nb, m, T)
            attn[hd] = jnp.concatenate([a3[:, :m, :], upd], axis=1).reshape(T, T)

    for hd in heads:
        o = inter[hd] + _dot(attn[hd].astype(BF16), v[hd])
        o = _rms(o) * gain_ref[:, col(hd)] * gate_ref[:, col(hd)].astype(F32)
        o_ref[:, col(hd)] = o.astype(o_ref.dtype)


def _hgrn(proj, b, out_gain):
    S = proj.shape[0]
    T, hp = HG_BLOCK, HG_HEADS_PER_STEP
    w = hp * HDIM
    col = lambda off: pl.BlockSpec((T, w), lambda h, i: (i, off // hp + h))
    return pl.pallas_call(
        _hgrn_kernel,
        grid=(HEADS // hp, S // T),
        in_specs=[
            col(COL_Q_HG), col(COL_K_HG), col(COL_V_HG), col(0), col(COL_GATE),
            pl.BlockSpec((1, w), lambda h, i: (0, h)),
        ],
        out_specs=pl.BlockSpec((T, w), lambda h, i: (i, h)),
        out_shape=jax.ShapeDtypeStruct((S, GROUP_W), BF16),
        scratch_shapes=[pltpu.VMEM((hp, HDIM, HDIM), F32)],
        compiler_params=pltpu.CompilerParams(
            dimension_semantics=("arbitrary", "arbitrary"), vmem_limit_bytes=VMEM_LIMIT),
        name="hgrn2",
    )(proj, proj, proj, b, proj, out_gain)


def _bias_kernel(row_ref, o_ref):
    x = jnp.broadcast_to(row_ref[0], (ATT_SUB, BIAS_EXT))
    y = pltpu.roll(x, 0, 1, stride=1, stride_axis=0)[:, :ATT_KEYS]
    r = lax.broadcasted_iota(jnp.int32, (ATT_SUB, ATT_KEYS), 0)
    u = lax.broadcasted_iota(jnp.int32, (ATT_SUB, ATT_KEYS), 1)
    lo = jnp.where(r < CHUNK, 0, CHUNK)
    visible = (u >= lo) & (u < lo + BAND_CHUNKS * CHUNK)
    o_ref[0] = jnp.where(visible, y, -jnp.inf)


def _rel_bias(rel_bias):
    far = rel_bias[:, 2 * REL_CLIP:]
    pad = BAND_CHUNKS * CHUNK - REL_CLIP - CHUNK
    ext = jnp.concatenate([
        jnp.broadcast_to(far, (HEADS, pad)),
        rel_bias[:, :0:-1],
        jnp.broadcast_to(far, (HEADS, BIAS_EXT - pad - 2 * REL_CLIP)),
    ], axis=1).reshape(HEADS, 1, BIAS_EXT)
    return pl.pallas_call(
        _bias_kernel,
        grid=(HEADS,),
        in_specs=[pl.BlockSpec((1, 1, BIAS_EXT), lambda h: (h, 0, 0))],
        out_specs=pl.BlockSpec((1, ATT_SUB, ATT_KEYS), lambda h: (h, 0, 0)),
        out_shape=jax.ShapeDtypeStruct((HEADS, ATT_SUB, ATT_KEYS), F32),
        name="rel_bias",
    )(ext)


def _attn_kernel(bias_ref, q_ref, kp_ref, kc_ref, vp_ref, vc_ref, o_ref):
    first = pl.program_id(1) == 0
    kcat = jnp.concatenate([kp_ref[...], kc_ref[...]], axis=0)
    vcat = jnp.concatenate([vp_ref[...], vc_ref[...]], axis=0)
    bias = bias_ref[0]
    u = lax.broadcasted_iota(jnp.int32, (ATT_SUB, ATT_KEYS), 1)
    for j in range(ATT_BLOCK // ATT_SUB):
        rows = slice(j * ATT_SUB, (j + 1) * ATT_SUB)
        keys = slice(j * ATT_SUB, j * ATT_SUB + ATT_KEYS)
        s = _dot_nt(q_ref[rows, :], kcat[keys]) + bias
        n_before = jnp.where(first, ATT_BLOCK - j * ATT_SUB, 0)
        s = jnp.where(u < n_before, -jnp.inf, s)
        p = jnp.exp(s - jnp.max(s, axis=-1, keepdims=True))
        o = _dot(p.astype(BF16), vcat[keys]) / jnp.sum(p, axis=-1, keepdims=True)
        o_ref[rows, :] = o.astype(o_ref.dtype)


def _attention(proj, bias):
    S = proj.shape[0]
    cur = lambda off: pl.BlockSpec((ATT_BLOCK, HDIM), lambda h, i: (i, off + h))
    prev = lambda off: pl.BlockSpec(
        (ATT_BLOCK, HDIM), lambda h, i: (jnp.maximum(i - 1, 0), off + h))
    return pl.pallas_call(
        _attn_kernel,
        grid=(HEADS, S // ATT_BLOCK),
        in_specs=[
            pl.BlockSpec((1, ATT_SUB, ATT_KEYS), lambda h, i: (h, 0, 0)),
            cur(COL_Q_AT), prev(COL_K_AT), cur(COL_K_AT), prev(COL_V_AT), cur(COL_V_AT),
        ],
        out_specs=pl.BlockSpec((ATT_BLOCK, HDIM), lambda h, i: (i, h)),
        out_shape=jax.ShapeDtypeStruct((S, GROUP_W), BF16),
        compiler_params=pltpu.CompilerParams(
            dimension_semantics=("arbitrary", "arbitrary"), vmem_limit_bytes=VMEM_LIMIT),
        name="band_attn",
    )(bias, proj, proj, proj, proj, proj)


def _outproj_kernel(ohg_ref, oat_ref, x_ref, w_ref, again_ref, gffn_ref, x1_ref, h2_ref):
    oat = (_rms(oat_ref[...].astype(F32)) * again_ref[...]).astype(BF16)
    x1 = x_ref[...] + _dot(ohg_ref[...], w_ref[0:GROUP_W, :]) + _dot(oat, w_ref[GROUP_W:, :])
    x1_ref[...] = x1
    h2_ref[...] = (_rms(x1) * gffn_ref[...]).astype(BF16)


def _outproj(o_hg, o_at, x2, w_out, at_gain, g_ffn, tm):
    S, D = x2.shape
    full = lambda shape: pl.BlockSpec(shape, lambda i: (0, 0))
    rows = lambda w: pl.BlockSpec((tm, w), lambda i: (i, 0))
    return pl.pallas_call(
        _outproj_kernel,
        grid=(S // tm,),
        in_specs=[rows(GROUP_W), rows(GROUP_W), rows(D),
                  pl.BlockSpec(w_out.shape, lambda i: (0, 0), pipeline_mode=pl.Buffered(1)),
                  full((1, GROUP_W)), full((1, D))],
        out_specs=[rows(D), rows(D)],
        out_shape=[jax.ShapeDtypeStruct((S, D), F32), jax.ShapeDtypeStruct((S, D), BF16)],
        compiler_params=pltpu.CompilerParams(
            dimension_semantics=("arbitrary",), vmem_limit_bytes=VMEM_LIMIT),
        name="outproj",
    )(o_hg, o_at, x2, w_out, at_gain, g_ffn)


def _conv3(u, cw_ref, cb_ref):
    z = pltpu.roll(cw_ref[0:1, :] * u, 1, 0) + cw_ref[1:2, :] * u
    return pltpu.roll(z, 1, 0) + (cw_ref[2:3, :] * u + cb_ref[...])


def _ffn_up_kernel(h_ref, wa_ref, wg_ref, cwa_ref, cwg_ref, cba_ref, cbg_ref, o_ref,
                   ta_ref, tg_ref, *, cblk):
    tm, tn = o_ref.shape
    pad = SUBLANES

    @pl.when(pl.program_id(1) == 0)
    def _():
        ta_ref[...] = jnp.zeros_like(ta_ref)
        tg_ref[...] = jnp.zeros_like(tg_ref)

    def conv(u, t_ref, cw_ref, cb_ref, cols):
        cw = cw_ref.at[:, cols]
        cb = cb_ref.at[:, cols]
        y = _conv3(u, cw, cb)
        top = _conv3(jnp.concatenate([t_ref[:, cols], u[0:pad]], axis=0), cw, cb)
        t_ref[:, cols] = u[tm - pad:tm]
        return jnp.concatenate([top[pad:2 * pad], y[pad:]], axis=0)

    h = h_ref[...]
    for c0 in range(0, tn, cblk):
        cols = slice(c0, c0 + cblk)
        ya = conv(_dot(h, wa_ref[:, cols]), ta_ref, cwa_ref, cba_ref, cols)
        yg = conv(_dot(h, wg_ref[:, cols]), tg_ref, cwg_ref, cbg_ref, cols)
        o_ref[:, cols] = (ya * (yg * _sigmoid(yg))).astype(o_ref.dtype)


def _ffn_up(h2, w_up, conv_w, conv_b, tm, n_slabs, cblk):
    S, D = h2.shape
    d_ff = w_up.shape[1] // 2
    tn = d_ff // n_slabs
    a_col = lambda r, **kw: pl.BlockSpec((r, tn), lambda c, m: (0, c), **kw)
    g_col = lambda r, **kw: pl.BlockSpec((r, tn), lambda c, m: (0, n_slabs + c), **kw)
    once = dict(pipeline_mode=pl.Buffered(1))
    return pl.pallas_call(
        functools.partial(_ffn_up_kernel, cblk=cblk),
        grid=(n_slabs, S // tm),
        in_specs=[pl.BlockSpec((tm, D), lambda c, m: (m, 0)),
                  a_col(D, **once), g_col(D, **once),
                  a_col(CONV_W), g_col(CONV_W), a_col(1), g_col(1)],
        out_specs=pl.BlockSpec((tm, tn), lambda c, m: (m, c)),
        out_shape=jax.ShapeDtypeStruct((S, d_ff), BF16),
        scratch_shapes=[pltpu.VMEM((SUBLANES, tn), F32)] * 2,
        compiler_params=pltpu.CompilerParams(
            dimension_semantics=("arbitrary", "arbitrary"), vmem_limit_bytes=VMEM_LIMIT),
        name="ffn_up",
    )(h2, w_up, w_up, conv_w, conv_w, conv_b, conv_b)


def _ffn_down_kernel(a_ref, w_ref, x1_ref, o_ref):
    o_ref[...] = x1_ref[...] + _dot(a_ref[...], w_ref[...])


def _ffn_down(act, w_down, x1, tm):
    S, D = x1.shape
    d_ff = act.shape[1]
    return pl.pallas_call(
        _ffn_down_kernel,
        grid=(S // tm,),
        in_specs=[pl.BlockSpec((tm, d_ff), lambda i: (i, 0)),
                  pl.BlockSpec((d_ff, D), lambda i: (0, 0), pipeline_mode=pl.Buffered(1)),
                  pl.BlockSpec((tm, D), lambda i: (i, 0))],
        out_specs=pl.BlockSpec((tm, D), lambda i: (i, 0)),
        out_shape=jax.ShapeDtypeStruct((S, D), F32),
        compiler_params=pltpu.CompilerParams(
            dimension_semantics=("arbitrary",), vmem_limit_bytes=VMEM_LIMIT),
        name="ffn_down",
    )(act, w_down, x1)


def kernel(x, g_mix, w_in, hg_lb, hg_out_gain, q_gain, k_gain, rel_bias, at_out_gain, w_out,
           g_ffn, w_up, conv_w, conv_b, w_down):
    B, S, D = x.shape
    depth = g_mix.shape[0]
    assert B == 1 and depth == 1 and S % 1024 == 0
    assert w_in.shape[2] == N_GROUPS * GROUP_W
    x2 = x.reshape(S, D)
    proj, b = _inproj(x2, g_mix, w_in[0].astype(BF16), hg_lb, q_gain, k_gain, tm=HG_BLOCK)
    o_hg = _hgrn(proj, b, hg_out_gain)
    o_at = _attention(proj, _rel_bias(rel_bias[0]))
    x1, h2 = _outproj(o_hg, o_at, x2, w_out[0].astype(BF16), at_out_gain, g_ffn, tm=512)
    act = _ffn_up(h2, w_up[0].astype(BF16), conv_w[0], conv_b, tm=256, n_slabs=2, cblk=256)
    out = _ffn_down(act, w_down[0].astype(BF16), x1, tm=512)
    return out.reshape(B, S, D)
```

```python
import functools

import jax
import jax.numpy as jnp
import numpy as np
from jax import lax
from jax.experimental import pallas as pl
from jax.experimental.pallas import tpu as pltpu

F32 = jnp.float32
BF16 = jnp.bfloat16

CHUNK = 64
EPS = 1e-6
HEADS = 8
HDIM = 128
GROUP_W = HEADS * HDIM
N_GROUPS = 7
BAND_CHUNKS = 9
REL_CLIP = 128
CONV_W = 3
SUBLANES = 8

COL_Q_HG, COL_K_HG, COL_V_HG, COL_GATE, COL_Q_AT, COL_K_AT, COL_V_AT = (
    0, 8, 16, 24, 32, 40, 48)

ATT_BLOCK = 512
ATT_SUB = 128
ATT_KEYS = ATT_SUB + (BAND_CHUNKS - 1) * CHUNK
BIAS_EXT = 768

VMEM_LIMIT = 56 * 1024 * 1024


def _dot(a, b):
    return jnp.dot(a, b, preferred_element_type=F32)


def _dot_nt(a, b):
    return lax.dot_general(a, b, (((1,), (1,)), ((), ())), preferred_element_type=F32)


def _dot_tn(a, b):
    return lax.dot_general(a, b, (((0,), (0,)), ((), ())), preferred_element_type=F32)


def _sigmoid(x):
    return 1.0 / (1.0 + jnp.exp(-x))


def _tril(n):
    return jnp.asarray(np.tril(np.ones((n, n), np.float32)), BF16)


def _rms(x):
    return x * lax.rsqrt(jnp.mean(x * x, axis=-1, keepdims=True) + EPS)


def _inproj_kernel(x_ref, gmix_ref, w_ref, lbp_ref, qg_ref, kg_ref, tri_ref, out_ref, b_ref):
    h = (_rms(x_ref[...]) * gmix_ref[...]).astype(BF16)

    def proj(j):
        return _dot(h, w_ref[:, j * GROUP_W:(j + 1) * GROUP_W])

    def put(j, val):
        out_ref[:, j * GROUP_W:(j + 1) * GROUP_W] = val.astype(BF16)

    a = proj(0)
    put(0, a * _sigmoid(a))

    a = proj(1)
    p0 = lbp_ref[0:1, :]
    p1 = lbp_ref[1:2, :]
    pm = jnp.maximum(p0, p1)
    e0 = jnp.exp(p0 - pm)
    e1 = jnp.exp(p1 - pm)
    lb = e0 / (e0 + e1)
    s = _sigmoid(a)
    put(1, (1.0 - lb) * (1.0 - s))
    g = jnp.log2(lb + (1.0 - lb) * s)
    g_hi = g.astype(BF16)
    g_lo = (g - g_hi.astype(F32)).astype(BF16)
    b_ref[...] = _dot(tri_ref[...], g_hi) + _dot(tri_ref[...], g_lo)

    put(2, proj(2))

    a = proj(3)
    put(3, a * _sigmoid(a))

    scale = HDIM ** -0.5
    for j, gain_ref, mul in ((4, qg_ref, scale), (5, kg_ref, 1.0)):
        a = proj(j)
        gain = gain_ref[...] * mul
        for hd in range(HEADS):
            ah = a[:, hd * HDIM:(hd + 1) * HDIM]
            c0 = j * GROUP_W + hd * HDIM
            out_ref[:, c0:c0 + HDIM] = (_rms(ah) * gain).astype(BF16)

    put(6, proj(6))


def _inproj(x2, g_mix, w_in, hg_lb, q_gain, k_gain, tm):
    S, D = x2.shape
    n_out = w_in.shape[1]
    full = lambda shape: pl.BlockSpec(shape, lambda i: (0, 0))
    return pl.pallas_call(
        _inproj_kernel,
        grid=(S // tm,),
        in_specs=[
            pl.BlockSpec((tm, D), lambda i: (i, 0)),
            full((1, D)),
            pl.BlockSpec((D, n_out), lambda i: (0, 0), pipeline_mode=pl.Buffered(1)),
            full(hg_lb.shape),
            full((1, HDIM)),
            full((1, HDIM)),
            full((tm, tm)),
        ],
        out_specs=[
            pl.BlockSpec((tm, n_out), lambda i: (i, 0)),
            pl.BlockSpec((tm, GROUP_W), lambda i: (i, 0)),
        ],
        out_shape=[
            jax.ShapeDtypeStruct((S, n_out), BF16),
            jax.ShapeDtypeStruct((S, GROUP_W), F32),
        ],
        compiler_params=pltpu.CompilerParams(
            dimension_semantics=("arbitrary",), vmem_limit_bytes=VMEM_LIMIT),
        name="inproj",
    )(x2, g_mix, w_in, hg_lb, q_gain, k_gain, _tril(tm))


HG_BLOCK = 256
HG_HEADS_PER_STEP = 8
HG_HIGH = (128, 64, 32, 16, 8)
HG_LOW = (4, 2, 1)


def _hgrn_kernel(mlow_ref, q_ref, k_ref, v_ref, b_ref, gate_ref, gain_ref, o_ref, state_ref):
    T = HG_BLOCK
    H = T // 2
    NH = HG_HEADS_PER_STEP

    @pl.when(pl.program_id(1) == 0)
    def _():
        state_ref[...] = jnp.zeros_like(state_ref)

    row = lax.broadcasted_iota(jnp.int32, (T, HDIM), 0)
    tt = lax.broadcasted_iota(jnp.int32, (T, T), 0)
    ss = lax.broadcasted_iota(jnp.int32, (T, T), 1)
    hs = lax.broadcasted_iota(jnp.int32, (H, T), 1)
    diag = tt == ss
    same2, same4, same8 = ((tt // n) == (ss // n) for n in (2, 4, 8))
    heads = range(NH)
    col = lambda hd: slice(hd * HDIM, (hd + 1) * HDIM)
    second_half = [mlow_ref[li] for li in range(len(HG_LOW))]
    first_half = [1.0 - mk for mk in second_half]
    q = [q_ref[:, col(hd)] for hd in heads]
    k = [k_ref[:, col(hd)] for hd in heads]
    v = [v_ref[:, col(hd)] for hd in heads]
    b = [b_ref[:, col(hd)] for hd in heads]

    inter = []
    for hd in heads:
        state = state_ref[hd]
        b_last = b[hd][T - 1:T, :]
        inter.append(_dot_nt(q[hd] * jnp.exp2(b[hd]).astype(BF16), state.astype(BF16)))
        k_out = k[hd] * jnp.exp2(b_last - b[hd]).astype(BF16)
        state_ref[hd] = state * jnp.exp2(b_last) + _dot_tn(v[hd], k_out)

    attn = []
    for hd in heads:
        bh = b[hd]
        prev1 = bh - pltpu.roll(bh, 1, 0)
        prev2 = bh - pltpu.roll(bh, 2, 0)
        next1 = pltpu.roll(bh, T - 1, 0) - bh
        b3 = bh.reshape(T // SUBLANES, SUBLANES, HDIM)
        mid = jnp.broadcast_to(b3[:, 3:4, :], b3.shape).reshape(T, HDIM)
        pos = row & 3
        exps = {
            4: jnp.where((row & 4) != 0, bh - mid, mid - bh),
            2: jnp.where(pos == 2, prev1, jnp.where(pos == 3, prev2,
                                                   jnp.where(pos == 0, next1, 0.0))),
            1: jnp.where((row & 1) != 0, prev1, 0.0),
        }
        prods = {}
        for li, m in enumerate(HG_LOW):
            dec = jnp.exp2(exps[m]).astype(BF16)
            q_m = q[hd] * (dec * second_half[li])
            k_m = k[hd] * (dec * first_half[li])
            prods[m] = _dot_nt(q_m, k_m)
        pd = _dot_nt(q[hd], k[hd])
        attn.append(jnp.where(same2, jnp.where(diag, pd, prods[1]),
                              jnp.where(same4, prods[2], jnp.where(same8, prods[4], 0.0))))

    for m in HG_HIGH:
        nb = T // (2 * m)
        split = lambda a: a.reshape(nb, 2 * m, HDIM)
        second = lambda a: split(a)[:, m:, :].reshape(H, HDIM)
        keep = ((lax.broadcasted_iota(jnp.int32, (H, T), 0) // m) == (hs // (2 * m))) & (
            (hs & m) == 0)
        for hd in heads:
            b3 = split(b[hd])
            mid = b3[:, m - 1:m, :]
            dec_b = jnp.exp2(b3[:, m:, :] - mid).reshape(H, HDIM)
            dec_a = jnp.concatenate(
                [jnp.exp2(mid - b3[:, :m, :]), jnp.zeros((nb, m, HDIM), F32)], axis=1
            ).reshape(T, HDIM)
            if m >= 2 * SUBLANES:
                q_b = second(q[hd]) * dec_b.astype(BF16)
            else:
                q_b = (second(q[hd].astype(F32)) * dec_b).astype(BF16)
            k_a = k[hd] * dec_a.astype(BF16)
            p = _dot_nt(q_b, k_a)
            a3 = attn[hd].reshape(nb, 2 * m, T)
            upd = jnp.where(keep, p, a3[:, m:, :].reshape(H, T)).reshape(nb, m, T)
            attn[hd] = jnp.concatenate([a3[:, :m, :], upd], axis=1).reshape(T, T)

    for hd in heads:
        o = inter[hd] + _dot(attn[hd].astype(BF16), v[hd])
        o = (_rms(o) * gain_ref[:, col(hd)]).astype(BF16) * gate_ref[:, col(hd)]
        o_ref[:, col(hd)] = o.astype(o_ref.dtype)


def _hgrn(proj, b, out_gain):
    S = proj.shape[0]
    T, hp = HG_BLOCK, HG_HEADS_PER_STEP
    w = hp * HDIM
    col = lambda off: pl.BlockSpec((T, w), lambda h, i: (i, off // hp + h))
    t = np.arange(T)
    mlow = jnp.asarray(np.stack([np.broadcast_to(((t & m) != 0)[:, None], (T, HDIM))
                                 for m in HG_LOW]).astype(np.float32), BF16)
    return pl.pallas_call(
        _hgrn_kernel,
        grid=(HEADS // hp, S // T),
        in_specs=[
            pl.BlockSpec(mlow.shape, lambda h, i: (0, 0, 0)),
            col(COL_Q_HG), col(COL_K_HG), col(COL_V_HG), col(0), col(COL_GATE),
            pl.BlockSpec((1, w), lambda h, i: (0, h)),
        ],
        out_specs=pl.BlockSpec((T, w), lambda h, i: (i, h)),
        out_shape=jax.ShapeDtypeStruct((S, GROUP_W), BF16),
        scratch_shapes=[pltpu.VMEM((hp, HDIM, HDIM), F32)],
        compiler_params=pltpu.CompilerParams(
            dimension_semantics=("arbitrary", "arbitrary"), vmem_limit_bytes=VMEM_LIMIT),
        name="hgrn2",
    )(mlow, proj, proj, proj, b, proj, out_gain)


def _bias_kernel(row_ref, o_ref):
    x = jnp.broadcast_to(row_ref[0], (ATT_SUB, BIAS_EXT))
    y = pltpu.roll(x, 0, 1, stride=1, stride_axis=0)[:, :ATT_KEYS]
    r = lax.broadcasted_iota(jnp.int32, (ATT_SUB, ATT_KEYS), 0)
    u = lax.broadcasted_iota(jnp.int32, (ATT_SUB, ATT_KEYS), 1)
    lo = jnp.where(r < CHUNK, 0, CHUNK)
    visible = (u >= lo) & (u < lo + BAND_CHUNKS * CHUNK)
    o_ref[0] = jnp.where(visible, y, -jnp.inf)


def _rel_bias(rel_bias):
    far = rel_bias[:, 2 * REL_CLIP:]
    pad = BAND_CHUNKS * CHUNK - REL_CLIP - CHUNK
    ext = jnp.concatenate([
        jnp.broadcast_to(far, (HEADS, pad)),
        rel_bias[:, :0:-1],
        jnp.broadcast_to(far, (HEADS, BIAS_EXT - pad - 2 * REL_CLIP)),
    ], axis=1).reshape(HEADS, 1, BIAS_EXT)
    return pl.pallas_call(
        _bias_kernel,
        grid=(HEADS,),
        in_specs=[pl.BlockSpec((1, 1, BIAS_EXT), lambda h: (h, 0, 0))],
        out_specs=pl.BlockSpec((1, ATT_SUB, ATT_KEYS), lambda h: (h, 0, 0)),
        out_shape=jax.ShapeDtypeStruct((HEADS, ATT_SUB, ATT_KEYS), F32),
        name="rel_bias",
    )(ext)


ATT_HEADS_PER_STEP = 4


def _attn_body(bias_ref, q_ref, kp_ref, kc_ref, vp_ref, vc_ref, o_ref, first):
    n_sub = ATT_BLOCK // ATT_SUB
    u = lax.broadcasted_iota(jnp.int32, (ATT_SUB, ATT_KEYS), 1)
    jobs = [(hd, j) for hd in range(ATT_HEADS_PER_STEP) for j in range(n_sub)]
    col = lambda hd: slice(hd * HDIM, (hd + 1) * HDIM)
    rows = lambda j: slice(j * ATT_SUB, (j + 1) * ATT_SUB)
    keys = lambda j: slice(j * ATT_SUB, j * ATT_SUB + ATT_KEYS)
    kcat = [jnp.concatenate([kp_ref[:, col(hd)], kc_ref[:, col(hd)]], axis=0)
            for hd in range(ATT_HEADS_PER_STEP)]
    vcat = [jnp.concatenate([vp_ref[:, col(hd)], vc_ref[:, col(hd)]], axis=0)
            for hd in range(ATT_HEADS_PER_STEP)]

    scores = []
    for hd, j in jobs:
        s = _dot_nt(q_ref[rows(j), col(hd)], kcat[hd][keys(j)]) + bias_ref[hd]
        if first:
            s = jnp.where(u < ATT_BLOCK - j * ATT_SUB, -jnp.inf, s)
        scores.append(s)
    probs, sums = [], []
    for s in scores:
        p = jnp.exp(s - jnp.max(s, axis=-1, keepdims=True))
        sums.append(jnp.sum(p, axis=-1, keepdims=True))
        probs.append(p.astype(BF16))
    for (hd, j), p, l in zip(jobs, probs, sums):
        o = _dot(p, vcat[hd][keys(j)]) / l
        o_ref[rows(j), col(hd)] = o.astype(o_ref.dtype)


def _attn_kernel(*refs):
    first = pl.program_id(1) == 0

    @pl.when(first)
    def _():
        _attn_body(*refs, first=True)

    @pl.when(jnp.logical_not(first))
    def _():
        _attn_body(*refs, first=False)


def _attention(proj, bias):
    S = proj.shape[0]
    hp = ATT_HEADS_PER_STEP
    w = hp * HDIM
    cur = lambda off: pl.BlockSpec((ATT_BLOCK, w), lambda h, i: (i, off // hp + h))
    prev = lambda off: pl.BlockSpec(
        (ATT_BLOCK, w), lambda h, i: (jnp.maximum(i - 1, 0), off // hp + h))
    return pl.pallas_call(
        _attn_kernel,
        grid=(HEADS // hp, S // ATT_BLOCK),
        in_specs=[
            pl.BlockSpec((hp, ATT_SUB, ATT_KEYS), lambda h, i: (h, 0, 0)),
            cur(COL_Q_AT), prev(COL_K_AT), cur(COL_K_AT), prev(COL_V_AT), cur(COL_V_AT),
        ],
        out_specs=pl.BlockSpec((ATT_BLOCK, w), lambda h, i: (i, h)),
        out_shape=jax.ShapeDtypeStruct((S, GROUP_W), BF16),
        compiler_params=pltpu.CompilerParams(
            dimension_semantics=("arbitrary", "arbitrary"), vmem_limit_bytes=VMEM_LIMIT),
        name="band_attn",
    )(bias, proj, proj, proj, proj, proj)


def _outproj_kernel(ohg_ref, oat_ref, x_ref, w_ref, again_ref, gffn_ref, x1_ref, h2_ref, *, sub):
    for r0 in range(0, x_ref.shape[0], sub):
        rows = slice(r0, r0 + sub)
        oat = (_rms(oat_ref[rows, :].astype(F32)) * again_ref[...]).astype(BF16)
        x1 = (x_ref[rows, :] + _dot(ohg_ref[rows, :], w_ref[0:GROUP_W, :])
              + _dot(oat, w_ref[GROUP_W:, :]))
        x1_ref[rows, :] = x1
        h2_ref[rows, :] = (_rms(x1) * gffn_ref[...]).astype(BF16)


def _outproj(o_hg, o_at, x2, w_out, at_gain, g_ffn, tm):
    S, D = x2.shape
    full = lambda shape: pl.BlockSpec(shape, lambda i: (0, 0))
    rows = lambda w: pl.BlockSpec((tm, w), lambda i: (i, 0))
    return pl.pallas_call(
        functools.partial(_outproj_kernel, sub=tm // 2),
        grid=(S // tm,),
        in_specs=[rows(GROUP_W), rows(GROUP_W), rows(D),
                  pl.BlockSpec(w_out.shape, lambda i: (0, 0), pipeline_mode=pl.Buffered(1)),
                  full((1, GROUP_W)), full((1, D))],
        out_specs=[rows(D), rows(D)],
        out_shape=[jax.ShapeDtypeStruct((S, D), F32), jax.ShapeDtypeStruct((S, D), BF16)],
        compiler_params=pltpu.CompilerParams(
            dimension_semantics=("arbitrary",), vmem_limit_bytes=VMEM_LIMIT),
        name="outproj",
    )(o_hg, o_at, x2, w_out, at_gain, g_ffn)


def _conv3(u, cw_ref, cb_ref):
    z = pltpu.roll(cw_ref[0:1, :] * u, 1, 0) + cw_ref[1:2, :] * u
    return pltpu.roll(z, 1, 0) + (cw_ref[2:3, :] * u + cb_ref[...])


def _ffn_up_kernel(h_ref, wa_ref, wg_ref, cwa_ref, cwg_ref, cba_ref, cbg_ref, o_ref,
                   ta_ref, tg_ref, *, cblk):
    tm, tn = o_ref.shape
    pad = SUBLANES

    @pl.when(pl.program_id(1) == 0)
    def _():
        ta_ref[...] = jnp.zeros_like(ta_ref)
        tg_ref[...] = jnp.zeros_like(tg_ref)

    def conv(u, t_ref, cw_ref, cb_ref, cols):
        cw = cw_ref.at[:, cols]
        cb = cb_ref.at[:, cols]
        y = _conv3(u, cw, cb)
        top = _conv3(jnp.concatenate([t_ref[:, cols], u[0:pad]], axis=0), cw, cb)
        t_ref[:, cols] = u[tm - pad:tm]
        return jnp.concatenate([top[pad:2 * pad], y[pad:]], axis=0)

    h = h_ref[...]
    for c0 in range(0, tn, cblk):
        cols = slice(c0, c0 + cblk)
        ya = conv(_dot(h, wa_ref[:, cols]), ta_ref, cwa_ref, cba_ref, cols)
        yg = conv(_dot(h, wg_ref[:, cols]), tg_ref, cwg_ref, cbg_ref, cols)
        o_ref[:, cols] = (ya * (yg * _sigmoid(yg))).astype(o_ref.dtype)


def _ffn_up(h2, w_up, conv_w, conv_b, tm, n_slabs, cblk):
    S, D = h2.shape
    d_ff = w_up.shape[1] // 2
    tn = d_ff // n_slabs
    a_col = lambda r, **kw: pl.BlockSpec((r, tn), lambda c, m: (0, c), **kw)
    g_col = lambda r, **kw: pl.BlockSpec((r, tn), lambda c, m: (0, n_slabs + c), **kw)
    once = dict(pipeline_mode=pl.Buffered(1))
    return pl.pallas_call(
        functools.partial(_ffn_up_kernel, cblk=cblk),
        grid=(n_slabs, S // tm),
        in_specs=[pl.BlockSpec((tm, D), lambda c, m: (m, 0)),
                  a_col(D, **once), g_col(D, **once),
                  a_col(CONV_W), g_col(CONV_W), a_col(1), g_col(1)],
        out_specs=pl.BlockSpec((tm, tn), lambda c, m: (m, c)),
        out_shape=jax.ShapeDtypeStruct((S, d_ff), BF16),
        scratch_shapes=[pltpu.VMEM((SUBLANES, tn), F32)] * 2,
        compiler_params=pltpu.CompilerParams(
            dimension_semantics=("arbitrary", "arbitrary"), vmem_limit_bytes=VMEM_LIMIT),
        name="ffn_up",
    )(h2, w_up, w_up, conv_w, conv_w, conv_b, conv_b)


def _ffn_down_kernel(a_ref, w_ref, x1_ref, o_ref):
    o_ref[...] = x1_ref[...] + _dot(a_ref[...], w_ref[...])


def _ffn_down(act, w_down, x1, tm):
    S, D = x1.shape
    d_ff = act.shape[1]
    return pl.pallas_call(
        _ffn_down_kernel,
        grid=(S // tm,),
        in_specs=[pl.BlockSpec((tm, d_ff), lambda i: (i, 0)),
                  pl.BlockSpec((d_ff, D), lambda i: (0, 0), pipeline_mode=pl.Buffered(1)),
                  pl.BlockSpec((tm, D), lambda i: (i, 0))],
        out_specs=pl.BlockSpec((tm, D), lambda i: (i, 0)),
        out_shape=jax.ShapeDtypeStruct((S, D), F32),
        compiler_params=pltpu.CompilerParams(
            dimension_semantics=("arbitrary",), vmem_limit_bytes=VMEM_LIMIT),
        name="ffn_down",
    )(act, w_down, x1)


def kernel(x, g_mix, w_in, hg_lb, hg_out_gain, q_gain, k_gain, rel_bias, at_out_gain, w_out,
           g_ffn, w_up, conv_w, conv_b, w_down):
    B, S, D = x.shape
    depth = g_mix.shape[0]
    assert B == 1 and depth == 1 and S % 1024 == 0
    assert w_in.shape[2] == N_GROUPS * GROUP_W
    x2 = x.reshape(S, D)
    proj, b = _inproj(x2, g_mix, w_in[0].astype(BF16), hg_lb, q_gain, k_gain, tm=HG_BLOCK)
    o_hg = _hgrn(proj, b, hg_out_gain)
    o_at = _attention(proj, _rel_bias(rel_bias[0]))
    x1, h2 = _outproj(o_hg, o_at, x2, w_out[0].astype(BF16), at_out_gain, g_ffn, tm=512)
    act = _ffn_up(h2, w_up[0].astype(BF16), conv_w[0], conv_b, tm=256, n_slabs=2, cblk=256)
    out = _ffn_down(act, w_down[0].astype(BF16), x1, tm=512)
    return out.reshape(B, S, D)
```

```python
import functools

import jax
import jax.numpy as jnp
import numpy as np
from jax import lax
from jax.experimental import pallas as pl
from jax.experimental.pallas import tpu as pltpu

F32 = jnp.float32
BF16 = jnp.bfloat16

CHUNK = 64
EPS = 1e-6
HEADS = 8
HDIM = 128
GROUP_W = HEADS * HDIM
N_GROUPS = 7
BAND_CHUNKS = 9
REL_CLIP = 128
CONV_W = 3
SUBLANES = 8

COL_Q_HG, COL_K_HG, COL_V_HG, COL_GATE, COL_Q_AT, COL_K_AT, COL_V_AT = (
    0, 8, 16, 24, 32, 40, 48)

ATT_BLOCK = 512
ATT_SUB = 128
ATT_KEYS = ATT_SUB + (BAND_CHUNKS - 1) * CHUNK
BIAS_EXT = 768

VMEM_LIMIT = 56 * 1024 * 1024


def _dot(a, b):
    return jnp.dot(a, b, preferred_element_type=F32)


def _dot_nt(a, b):
    return lax.dot_general(a, b, (((1,), (1,)), ((), ())), preferred_element_type=F32)


def _dot_tn(a, b):
    return lax.dot_general(a, b, (((0,), (0,)), ((), ())), preferred_element_type=F32)


def _sigmoid(x):
    return 1.0 / (1.0 + jnp.exp(-x))


def _tril(n):
    return jnp.asarray(np.tril(np.ones((n, n), np.float32)), BF16)


def _rms(x):
    return x * lax.rsqrt(jnp.mean(x * x, axis=-1, keepdims=True) + EPS)


WEIGHT_STAGE_BYTES = 2 * 1024 * 1024


def _stage_rows(n_rows, n_cols):
    rows = max(SUBLANES, WEIGHT_STAGE_BYTES // (4 * n_cols))
    while n_rows % rows:
        rows //= 2
    return rows


def _weight_scratch(n_rows, n_cols):
    return [pltpu.VMEM((n_rows, n_cols), BF16),
            pltpu.VMEM((2, _stage_rows(n_rows, n_cols), n_cols), F32),
            pltpu.SemaphoreType.DMA((2,))]


def _load_weight(w_hbm, col0, w_ref, stage_ref, sem_ref):
    n_rows, n_cols = w_ref.shape
    rows = stage_ref.shape[1]

    def copy(c):
        return pltpu.make_async_copy(
            w_hbm.at[pl.ds(c * rows, rows), pl.ds(col0, n_cols)],
            stage_ref.at[c % 2], sem_ref.at[c % 2])

    n_chunks = n_rows // rows
    copy(0).start()
    for c in range(n_chunks):
        if c + 1 < n_chunks:
            copy(c + 1).start()
        copy(c).wait()
        w_ref[c * rows:(c + 1) * rows, :] = stage_ref[c % 2].astype(BF16)


def _inproj_kernel(x_ref, gmix_ref, w_hbm, lbp_ref, qg_ref, kg_ref, tri_ref, out_ref, b_ref,
                   w_ref, stage_ref, sem_ref):
    @pl.when(pl.program_id(0) == 0)
    def _():
        _load_weight(w_hbm, 0, w_ref, stage_ref, sem_ref)

    h = (_rms(x_ref[...]) * gmix_ref[...]).astype(BF16)

    def proj(j):
        return _dot(h, w_ref[:, j * GROUP_W:(j + 1) * GROUP_W])

    def put(j, val):
        out_ref[:, j * GROUP_W:(j + 1) * GROUP_W] = val.astype(BF16)

    a = proj(0)
    put(0, a * _sigmoid(a))

    a = proj(1)
    p0 = lbp_ref[0:1, :]
    p1 = lbp_ref[1:2, :]
    pm = jnp.maximum(p0, p1)
    e0 = jnp.exp(p0 - pm)
    e1 = jnp.exp(p1 - pm)
    lb = e0 / (e0 + e1)
    s = _sigmoid(a)
    put(1, (1.0 - lb) * (1.0 - s))
    g = jnp.log2(lb + (1.0 - lb) * s)
    g_hi = g.astype(BF16)
    g_lo = (g - g_hi.astype(F32)).astype(BF16)
    b_ref[...] = _dot(tri_ref[...], g_hi) + _dot(tri_ref[...], g_lo)

    put(2, proj(2))

    a = proj(3)
    put(3, a * _sigmoid(a))

    scale = HDIM ** -0.5
    for j, gain_ref, mul in ((4, qg_ref, scale), (5, kg_ref, 1.0)):
        a = proj(j)
        gain = gain_ref[...] * mul
        for hd in range(HEADS):
            ah = a[:, hd * HDIM:(hd + 1) * HDIM]
            c0 = j * GROUP_W + hd * HDIM
            out_ref[:, c0:c0 + HDIM] = (_rms(ah) * gain).astype(BF16)

    put(6, proj(6))


def _inproj(x2, g_mix, w_in, hg_lb, q_gain, k_gain, tm):
    S, D = x2.shape
    n_out = w_in.shape[1]
    full = lambda shape: pl.BlockSpec(shape, lambda i: (0, 0))
    return pl.pallas_call(
        _inproj_kernel,
        grid=(S // tm,),
        in_specs=[
            pl.BlockSpec((tm, D), lambda i: (i, 0)),
            full((1, D)),
            pl.BlockSpec(memory_space=pl.ANY),
            full(hg_lb.shape),
            full((1, HDIM)),
            full((1, HDIM)),
            full((tm, tm)),
        ],
        out_specs=[
            pl.BlockSpec((tm, n_out), lambda i: (i, 0)),
            pl.BlockSpec((tm, GROUP_W), lambda i: (i, 0)),
        ],
        out_shape=[
            jax.ShapeDtypeStruct((S, n_out), BF16),
            jax.ShapeDtypeStruct((S, GROUP_W), F32),
        ],
        scratch_shapes=_weight_scratch(D, n_out),
        compiler_params=pltpu.CompilerParams(
            dimension_semantics=("arbitrary",), vmem_limit_bytes=VMEM_LIMIT),
        name="inproj",
    )(x2, g_mix, w_in, hg_lb, q_gain, k_gain, _tril(tm))


HG_BLOCK = 256
HG_HEADS_PER_STEP = 8
HG_HIGH = (128, 64, 32, 16, 8)
HG_LOW = (4, 2, 1)


def _hgrn_kernel(mlow_ref, q_ref, k_ref, v_ref, b_ref, gate_ref, gain_ref, o_ref, state_ref):
    T = HG_BLOCK
    H = T // 2
    NH = HG_HEADS_PER_STEP

    @pl.when(pl.program_id(1) == 0)
    def _():
        state_ref[...] = jnp.zeros_like(state_ref)

    row = lax.broadcasted_iota(jnp.int32, (T, HDIM), 0)
    tt = lax.broadcasted_iota(jnp.int32, (T, T), 0)
    ss = lax.broadcasted_iota(jnp.int32, (T, T), 1)
    hs = lax.broadcasted_iota(jnp.int32, (H, T), 1)
    diag = tt == ss
    same2, same4, same8 = ((tt // n) == (ss // n) for n in (2, 4, 8))
    heads = range(NH)
    col = lambda hd: slice(hd * HDIM, (hd + 1) * HDIM)
    second_half = [mlow_ref[li] for li in range(len(HG_LOW))]
    first_half = [1.0 - mk for mk in second_half]
    q = [q_ref[:, col(hd)] for hd in heads]
    k = [k_ref[:, col(hd)] for hd in heads]
    v = [v_ref[:, col(hd)] for hd in heads]
    b = [b_ref[:, col(hd)] for hd in heads]

    inter = []
    for hd in heads:
        state = state_ref[hd]
        b_last = b[hd][T - 1:T, :]
        inter.append(_dot_nt(q[hd] * jnp.exp2(b[hd]).astype(BF16), state.astype(BF16)))
        k_out = k[hd] * jnp.exp2(b_last - b[hd]).astype(BF16)
        state_ref[hd] = state * jnp.exp2(b_last) + _dot_tn(v[hd], k_out)

    attn = []
    for hd in heads:
        bh = b[hd]
        prev1 = bh - pltpu.roll(bh, 1, 0)
        prev2 = bh - pltpu.roll(bh, 2, 0)
        next1 = pltpu.roll(bh, T - 1, 0) - bh
        b3 = bh.reshape(T // SUBLANES, SUBLANES, HDIM)
        mid = jnp.broadcast_to(b3[:, 3:4, :], b3.shape).reshape(T, HDIM)
        pos = row & 3
        exps = {
            4: jnp.where((row & 4) != 0, bh - mid, mid - bh),
            2: jnp.where(pos == 2, prev1, jnp.where(pos == 3, prev2,
                                                   jnp.where(pos == 0, next1, 0.0))),
            1: jnp.where((row & 1) != 0, prev1, 0.0),
        }
        prods = {}
        for li, m in enumerate(HG_LOW):
            dec = jnp.exp2(exps[m]).astype(BF16)
            q_m = q[hd] * (dec * second_half[li])
            k_m = k[hd] * (dec * first_half[li])
            prods[m] = _dot_nt(q_m, k_m)
        pd = _dot_nt(q[hd], k[hd])
        attn.append(jnp.where(same2, jnp.where(diag, pd, prods[1]),
                              jnp.where(same4, prods[2], jnp.where(same8, prods[4], 0.0))))

    for m in HG_HIGH:
        nb = T // (2 * m)
        split = lambda a: a.reshape(nb, 2 * m, HDIM)
        second = lambda a: split(a)[:, m:, :].reshape(H, HDIM)
        keep = ((lax.broadcasted_iota(jnp.int32, (H, T), 0) // m) == (hs // (2 * m))) & (
            (hs & m) == 0)
        for hd in heads:
            b3 = split(b[hd])
            mid = b3[:, m - 1:m, :]
            dec_b = jnp.exp2(b3[:, m:, :] - mid).reshape(H, HDIM)
            dec_a = jnp.concatenate(
                [jnp.exp2(mid - b3[:, :m, :]), jnp.zeros((nb, m, HDIM), F32)], axis=1
            ).reshape(T, HDIM)
            if m >= 2 * SUBLANES:
                q_b = second(q[hd]) * dec_b.astype(BF16)
            else:
                q_b = (second(q[hd].astype(F32)) * dec_b).astype(BF16)
            k_a = k[hd] * dec_a.astype(BF16)
            p = _dot_nt(q_b, k_a)
            a3 = attn[hd].reshape(nb, 2 * m, T)
            upd = jnp.where(keep, p, a3[:, m:, :].reshape(H, T)).reshape(nb, m, T)
            attn[hd] = jnp.concatenate([a3[:, :m, :], upd], axis=1).reshape(T, T)

    for hd in heads:
        o = inter[hd] + _dot(attn[hd].astype(BF16), v[hd])
        o = (_rms(o) * gain_ref[:, col(hd)]).astype(BF16) * gate_ref[:, col(hd)]
        o_ref[:, col(hd)] = o.astype(o_ref.dtype)


def _hgrn(proj, b, out_gain):
    S = proj.shape[0]
    T, hp = HG_BLOCK, HG_HEADS_PER_STEP
    w = hp * HDIM
    col = lambda off: pl.BlockSpec((T, w), lambda h, i: (i, off // hp + h))
    t = np.arange(T)
    mlow = jnp.asarray(np.stack([np.broadcast_to(((t & m) != 0)[:, None], (T, HDIM))
                                 for m in HG_LOW]).astype(np.float32), BF16)
    return pl.pallas_call(
        _hgrn_kernel,
        grid=(HEADS // hp, S // T),
        in_specs=[
            pl.BlockSpec(mlow.shape, lambda h, i: (0, 0, 0)),
            col(COL_Q_HG), col(COL_K_HG), col(COL_V_HG), col(0), col(COL_GATE),
            pl.BlockSpec((1, w), lambda h, i: (0, h)),
        ],
        out_specs=pl.BlockSpec((T, w), lambda h, i: (i, h)),
        out_shape=jax.ShapeDtypeStruct((S, GROUP_W), BF16),
        scratch_shapes=[pltpu.VMEM((hp, HDIM, HDIM), F32)],
        compiler_params=pltpu.CompilerParams(
            dimension_semantics=("arbitrary", "arbitrary"), vmem_limit_bytes=VMEM_LIMIT),
        name="hgrn2",
    )(mlow, proj, proj, proj, b, proj, out_gain)


def _bias_kernel(row_ref, o_ref):
    x = jnp.broadcast_to(row_ref[0], (ATT_SUB, BIAS_EXT))
    y = pltpu.roll(x, 0, 1, stride=1, stride_axis=0)[:, :ATT_KEYS]
    r = lax.broadcasted_iota(jnp.int32, (ATT_SUB, ATT_KEYS), 0)
    u = lax.broadcasted_iota(jnp.int32, (ATT_SUB, ATT_KEYS), 1)
    lo = jnp.where(r < CHUNK, 0, CHUNK)
    visible = (u >= lo) & (u < lo + BAND_CHUNKS * CHUNK)
    o_ref[0] = jnp.where(visible, y, -jnp.inf)


def _rel_bias(rel_bias):
    far = rel_bias[:, 2 * REL_CLIP:]
    pad = BAND_CHUNKS * CHUNK - REL_CLIP - CHUNK
    ext = jnp.concatenate([
        jnp.broadcast_to(far, (HEADS, pad)),
        rel_bias[:, :0:-1],
        jnp.broadcast_to(far, (HEADS, BIAS_EXT - pad - 2 * REL_CLIP)),
    ], axis=1).reshape(HEADS, 1, BIAS_EXT)
    return pl.pallas_call(
        _bias_kernel,
        grid=(HEADS,),
        in_specs=[pl.BlockSpec((1, 1, BIAS_EXT), lambda h: (h, 0, 0))],
        out_specs=pl.BlockSpec((1, ATT_SUB, ATT_KEYS), lambda h: (h, 0, 0)),
        out_shape=jax.ShapeDtypeStruct((HEADS, ATT_SUB, ATT_KEYS), F32),
        name="rel_bias",
    )(ext)


ATT_HEADS_PER_STEP = 4


def _attn_body(bias_ref, q_ref, kp_ref, kc_ref, vp_ref, vc_ref, o_ref, first):
    n_sub = ATT_BLOCK // ATT_SUB
    u = lax.broadcasted_iota(jnp.int32, (ATT_SUB, ATT_KEYS), 1)
    jobs = [(hd, j) for hd in range(ATT_HEADS_PER_STEP) for j in range(n_sub)]
    col = lambda hd: slice(hd * HDIM, (hd + 1) * HDIM)
    rows = lambda j: slice(j * ATT_SUB, (j + 1) * ATT_SUB)
    keys = lambda j: slice(j * ATT_SUB, j * ATT_SUB + ATT_KEYS)
    kcat = [jnp.concatenate([kp_ref[:, col(hd)], kc_ref[:, col(hd)]], axis=0)
            for hd in range(ATT_HEADS_PER_STEP)]
    vcat = [jnp.concatenate([vp_ref[:, col(hd)], vc_ref[:, col(hd)]], axis=0)
            for hd in range(ATT_HEADS_PER_STEP)]

    scores = []
    for hd, j in jobs:
        s = _dot_nt(q_ref[rows(j), col(hd)], kcat[hd][keys(j)]) + bias_ref[hd]
        if first:
            s = jnp.where(u < ATT_BLOCK - j * ATT_SUB, -jnp.inf, s)
        scores.append(s)
    probs, sums = [], []
    for s in scores:
        p = jnp.exp(s - jnp.max(s, axis=-1, keepdims=True))
        sums.append(jnp.sum(p, axis=-1, keepdims=True))
        probs.append(p.astype(BF16))
    for (hd, j), p, l in zip(jobs, probs, sums):
        o = _dot(p, vcat[hd][keys(j)]) / l
        o_ref[rows(j), col(hd)] = o.astype(o_ref.dtype)


def _attn_kernel(*refs):
    first = pl.program_id(1) == 0

    @pl.when(first)
    def _():
        _attn_body(*refs, first=True)

    @pl.when(jnp.logical_not(first))
    def _():
        _attn_body(*refs, first=False)


def _attention(proj, bias):
    S = proj.shape[0]
    hp = ATT_HEADS_PER_STEP
    w = hp * HDIM
    cur = lambda off: pl.BlockSpec((ATT_BLOCK, w), lambda h, i: (i, off // hp + h))
    prev = lambda off: pl.BlockSpec(
        (ATT_BLOCK, w), lambda h, i: (jnp.maximum(i - 1, 0), off // hp + h))
    return pl.pallas_call(
        _attn_kernel,
        grid=(HEADS // hp, S // ATT_BLOCK),
        in_specs=[
            pl.BlockSpec((hp, ATT_SUB, ATT_KEYS), lambda h, i: (h, 0, 0)),
            cur(COL_Q_AT), prev(COL_K_AT), cur(COL_K_AT), prev(COL_V_AT), cur(COL_V_AT),
        ],
        out_specs=pl.BlockSpec((ATT_BLOCK, w), lambda h, i: (i, h)),
        out_shape=jax.ShapeDtypeStruct((S, GROUP_W), BF16),
        compiler_params=pltpu.CompilerParams(
            dimension_semantics=("arbitrary", "arbitrary"), vmem_limit_bytes=VMEM_LIMIT),
        name="band_attn",
    )(bias, proj, proj, proj, proj, proj)


def _outproj_kernel(ohg_ref, oat_ref, x_ref, w_hbm, again_ref, gffn_ref, x1_ref, h2_ref,
                    w_ref, stage_ref, sem_ref, *, sub):
    @pl.when(pl.program_id(0) == 0)
    def _():
        _load_weight(w_hbm, 0, w_ref, stage_ref, sem_ref)

    for r0 in range(0, x_ref.shape[0], sub):
        rows = slice(r0, r0 + sub)
        oat = (_rms(oat_ref[rows, :].astype(F32)) * again_ref[...]).astype(BF16)
        x1 = (x_ref[rows, :] + _dot(ohg_ref[rows, :], w_ref[0:GROUP_W, :])
              + _dot(oat, w_ref[GROUP_W:, :]))
        x1_ref[rows, :] = x1
        h2_ref[rows, :] = (_rms(x1) * gffn_ref[...]).astype(BF16)


def _outproj(o_hg, o_at, x2, w_out, at_gain, g_ffn, tm):
    S, D = x2.shape
    full = lambda shape: pl.BlockSpec(shape, lambda i: (0, 0))
    rows = lambda w: pl.BlockSpec((tm, w), lambda i: (i, 0))
    return pl.pallas_call(
        functools.partial(_outproj_kernel, sub=tm // 2),
        grid=(S // tm,),
        in_specs=[rows(GROUP_W), rows(GROUP_W), rows(D),
                  pl.BlockSpec(memory_space=pl.ANY),
                  full((1, GROUP_W)), full((1, D))],
        out_specs=[rows(D), rows(D)],
        out_shape=[jax.ShapeDtypeStruct((S, D), F32), jax.ShapeDtypeStruct((S, D), BF16)],
        scratch_shapes=_weight_scratch(*w_out.shape),
        compiler_params=pltpu.CompilerParams(
            dimension_semantics=("arbitrary",), vmem_limit_bytes=VMEM_LIMIT),
        name="outproj",
    )(o_hg, o_at, x2, w_out, at_gain, g_ffn)


def _conv3(u, cw, cb):
    z = pltpu.roll(cw[0:1, :] * u, 1, 0) + cw[1:2, :] * u
    return pltpu.roll(z, 1, 0) + (cw[2:3, :] * u + cb)


def _ffn_up_kernel(h_ref, w_hbm, cwa_ref, cwg_ref, cba_ref, cbg_ref, o_ref,
                   ta_ref, tg_ref, wa_ref, stage_a, sem_a, wg_ref, stage_g, sem_g, *, cblk):
    tm, tn = o_ref.shape
    pad = SUBLANES

    @pl.when(pl.program_id(1) == 0)
    def _():
        ta_ref[...] = jnp.zeros_like(ta_ref)
        tg_ref[...] = jnp.zeros_like(tg_ref)
        slab = pl.program_id(0)
        d_ff = w_hbm.shape[1] // 2
        _load_weight(w_hbm, pl.multiple_of(slab * tn, HDIM), wa_ref, stage_a, sem_a)
        _load_weight(w_hbm, pl.multiple_of(d_ff + slab * tn, HDIM), wg_ref, stage_g, sem_g)

    def conv(u, t_ref, cw_ref, cb_ref, cols, scale):
        cw = cw_ref[:, cols] * scale
        cb = cb_ref[:, cols] * scale
        y = _conv3(u, cw, cb)
        top = _conv3(jnp.concatenate([t_ref[:, cols], u[0:pad]], axis=0), cw, cb)
        t_ref[:, cols] = u[tm - pad:tm]
        return jnp.concatenate([top[pad:2 * pad], y[pad:]], axis=0)

    h = h_ref[...]
    for c0 in range(0, tn, cblk):
        cols = slice(c0, c0 + cblk)
        ya = conv(_dot(h, wa_ref[:, cols]), ta_ref, cwa_ref, cba_ref, cols, 1.0)
        yh = conv(_dot(h, wg_ref[:, cols]), tg_ref, cwg_ref, cbg_ref, cols, 0.5)
        o_ref[:, cols] = (ya * (yh * (1.0 + jnp.tanh(yh)))).astype(o_ref.dtype)


def _ffn_up(h2, w_up, conv_w, conv_b, tm, n_slabs, cblk):
    S, D = h2.shape
    d_ff = w_up.shape[1] // 2
    tn = d_ff // n_slabs
    a_col = lambda r: pl.BlockSpec((r, tn), lambda c, m: (0, c))
    g_col = lambda r: pl.BlockSpec((r, tn), lambda c, m: (0, n_slabs + c))
    return pl.pallas_call(
        functools.partial(_ffn_up_kernel, cblk=cblk),
        grid=(n_slabs, S // tm),
        in_specs=[pl.BlockSpec((tm, D), lambda c, m: (m, 0)),
                  pl.BlockSpec(memory_space=pl.ANY),
                  a_col(CONV_W), g_col(CONV_W), a_col(1), g_col(1)],
        out_specs=pl.BlockSpec((tm, tn), lambda c, m: (m, c)),
        out_shape=jax.ShapeDtypeStruct((S, d_ff), BF16),
        scratch_shapes=[pltpu.VMEM((SUBLANES, tn), F32)] * 2 + 2 * _weight_scratch(D, tn),
        compiler_params=pltpu.CompilerParams(
            dimension_semantics=("arbitrary", "arbitrary"), vmem_limit_bytes=VMEM_LIMIT),
        name="ffn_up",
    )(h2, w_up, conv_w, conv_w, conv_b, conv_b)


def _ffn_down_kernel(a_ref, w_hbm, x1_ref, o_ref, w_ref, stage_ref, sem_ref):
    @pl.when(pl.program_id(0) == 0)
    def _():
        _load_weight(w_hbm, 0, w_ref, stage_ref, sem_ref)

    o_ref[...] = x1_ref[...] + _dot(a_ref[...], w_ref[...])


def _ffn_down(act, w_down, x1, tm):
    S, D = x1.shape
    d_ff = act.shape[1]
    return pl.pallas_call(
        _ffn_down_kernel,
        grid=(S // tm,),
        in_specs=[pl.BlockSpec((tm, d_ff), lambda i: (i, 0)),
                  pl.BlockSpec(memory_space=pl.ANY),
                  pl.BlockSpec((tm, D), lambda i: (i, 0))],
        out_specs=pl.BlockSpec((tm, D), lambda i: (i, 0)),
        out_shape=jax.ShapeDtypeStruct((S, D), F32),
        scratch_shapes=_weight_scratch(d_ff, D),
        compiler_params=pltpu.CompilerParams(
            dimension_semantics=("arbitrary",), vmem_limit_bytes=VMEM_LIMIT),
        name="ffn_down",
    )(act, w_down, x1)


def kernel(x, g_mix, w_in, hg_lb, hg_out_gain, q_gain, k_gain, rel_bias, at_out_gain, w_out,
           g_ffn, w_up, conv_w, conv_b, w_down):
    B, S, D = x.shape
    depth = g_mix.shape[0]
    assert B == 1 and depth == 1 and S % 1024 == 0
    assert w_in.shape[2] == N_GROUPS * GROUP_W
    x2 = x.reshape(S, D)
    proj, b = _inproj(x2, g_mix, w_in[0], hg_lb, q_gain, k_gain, tm=HG_BLOCK)
    o_hg = _hgrn(proj, b, hg_out_gain)
    o_at = _attention(proj, _rel_bias(rel_bias[0]))
    x1, h2 = _outproj(o_hg, o_at, x2, w_out[0], at_out_gain, g_ffn, tm=512)
    act = _ffn_up(h2, w_up[0], conv_w[0], conv_b, tm=256, n_slabs=2, cblk=256)
    out = _ffn_down(act, w_down[0], x1, tm=512)
    return out.reshape(B, S, D)
```

```python
import functools

import jax
import jax.numpy as jnp
import numpy as np
from jax import lax
from jax.experimental import pallas as pl
from jax.experimental.pallas import tpu as pltpu

F32 = jnp.float32
BF16 = jnp.bfloat16

CHUNK = 64
EPS = 1e-6
HEADS = 8
HDIM = 128
GROUP_W = HEADS * HDIM
N_GROUPS = 7
BAND_CHUNKS = 9
REL_CLIP = 128
CONV_W = 3
SUBLANES = 8

COL_Q_HG, COL_K_HG, COL_V_HG, COL_GATE, COL_Q_AT, COL_K_AT, COL_V_AT = (
    0, 8, 16, 24, 32, 40, 48)

ATT_BLOCK = 512
ATT_SUB = 128
ATT_KEYS = ATT_SUB + (BAND_CHUNKS - 1) * CHUNK
BIAS_EXT = 768

VMEM_LIMIT = 56 * 1024 * 1024


def _dot(a, b):
    return jnp.dot(a, b, preferred_element_type=F32)


def _dot_nt(a, b):
    return lax.dot_general(a, b, (((1,), (1,)), ((), ())), preferred_element_type=F32)


def _dot_tn(a, b):
    return lax.dot_general(a, b, (((0,), (0,)), ((), ())), preferred_element_type=F32)


def _sigmoid(x):
    return 1.0 / (1.0 + jnp.exp(-x))


def _tril(n):
    return jnp.asarray(np.tril(np.ones((n, n), np.float32)), BF16)


def _rms(x):
    return x * lax.rsqrt(jnp.mean(x * x, axis=-1, keepdims=True) + EPS)


WEIGHT_STAGE_BYTES = 2 * 1024 * 1024
WEIGHT_PANEL = 1024


def _stage_shape(n_rows, n_cols):
    panel = WEIGHT_PANEL
    while n_cols % panel:
        panel -= HDIM
    rows = max(SUBLANES, WEIGHT_STAGE_BYTES // (4 * panel))
    while n_rows % rows:
        rows //= 2
    return rows, panel


def _weight_scratch(n_rows, n_cols):
    return [pltpu.VMEM((n_rows, n_cols), BF16),
            pltpu.VMEM((2,) + _stage_shape(n_rows, n_cols), F32),
            pltpu.SemaphoreType.DMA((2,))]


def _load_weight(w_hbm, col0, w_ref, stage_ref, sem_ref):
    n_rows, n_cols = w_ref.shape
    _, rows, panel = stage_ref.shape
    chunks = [(r0, c0) for c0 in range(0, n_cols, panel) for r0 in range(0, n_rows, rows)]

    def copy(i):
        r0, c0 = chunks[i]
        return pltpu.make_async_copy(
            w_hbm.at[pl.ds(r0, rows), pl.ds(col0 + c0, panel)],
            stage_ref.at[i % 2], sem_ref.at[i % 2])

    copy(0).start()
    for i, (r0, c0) in enumerate(chunks):
        if i + 1 < len(chunks):
            copy(i + 1).start()
        copy(i).wait()
        w_ref[r0:r0 + rows, c0:c0 + panel] = stage_ref[i % 2].astype(BF16)


def _inproj_kernel(x_ref, gmix_ref, w_hbm, lbp_ref, qg_ref, kg_ref, tri_ref, out_ref, b_ref,
                   w_ref, stage_ref, sem_ref):
    @pl.when(pl.program_id(0) == 0)
    def _():
        _load_weight(w_hbm, 0, w_ref, stage_ref, sem_ref)

    h = (_rms(x_ref[...]) * gmix_ref[...]).astype(BF16)

    def proj(j):
        return _dot(h, w_ref[:, j * GROUP_W:(j + 1) * GROUP_W])

    def put(j, val):
        out_ref[:, j * GROUP_W:(j + 1) * GROUP_W] = val.astype(BF16)

    a = proj(0)
    put(0, a * _sigmoid(a))

    a = proj(1)
    p0 = lbp_ref[0:1, :]
    p1 = lbp_ref[1:2, :]
    pm = jnp.maximum(p0, p1)
    e0 = jnp.exp(p0 - pm)
    e1 = jnp.exp(p1 - pm)
    lb = e0 / (e0 + e1)
    s = _sigmoid(a)
    put(1, (1.0 - lb) * (1.0 - s))
    g = jnp.log2(lb + (1.0 - lb) * s)
    g_hi = g.astype(BF16)
    g_lo = (g - g_hi.astype(F32)).astype(BF16)
    b_ref[...] = _dot(tri_ref[...], g_hi) + _dot(tri_ref[...], g_lo)

    put(2, proj(2))

    a = proj(3)
    put(3, a * _sigmoid(a))

    scale = HDIM ** -0.5
    for j, gain_ref, mul in ((4, qg_ref, scale), (5, kg_ref, 1.0)):
        a = proj(j)
        gain = gain_ref[...] * mul
        for hd in range(HEADS):
            ah = a[:, hd * HDIM:(hd + 1) * HDIM]
            c0 = j * GROUP_W + hd * HDIM
            out_ref[:, c0:c0 + HDIM] = (_rms(ah) * gain).astype(BF16)

    put(6, proj(6))


def _inproj(x2, g_mix, w_in, hg_lb, q_gain, k_gain, tm):
    S, D = x2.shape
    n_out = w_in.shape[1]
    full = lambda shape: pl.BlockSpec(shape, lambda i: (0, 0))
    return pl.pallas_call(
        _inproj_kernel,
        grid=(S // tm,),
        in_specs=[
            pl.BlockSpec((tm, D), lambda i: (i, 0)),
            full((1, D)),
            pl.BlockSpec(memory_space=pl.ANY),
            full(hg_lb.shape),
            full((1, HDIM)),
            full((1, HDIM)),
            full((tm, tm)),
        ],
        out_specs=[
            pl.BlockSpec((tm, n_out), lambda i: (i, 0)),
            pl.BlockSpec((tm, GROUP_W), lambda i: (i, 0)),
        ],
        out_shape=[
            jax.ShapeDtypeStruct((S, n_out), BF16),
            jax.ShapeDtypeStruct((S, GROUP_W), F32),
        ],
        scratch_shapes=_weight_scratch(D, n_out),
        compiler_params=pltpu.CompilerParams(
            dimension_semantics=("arbitrary",), vmem_limit_bytes=VMEM_LIMIT),
        name="inproj",
    )(x2, g_mix, w_in, hg_lb, q_gain, k_gain, _tril(tm))


HG_BLOCK = 256
HG_HEADS_PER_STEP = 8
HG_HIGH = (128, 64, 32, 16, 8)
HG_LOW = (4, 2, 1)


def _hgrn_kernel(mlow_ref, q_ref, k_ref, v_ref, b_ref, gate_ref, gain_ref, o_ref, state_ref):
    T = HG_BLOCK
    H = T // 2
    NH = HG_HEADS_PER_STEP

    @pl.when(pl.program_id(1) == 0)
    def _():
        state_ref[...] = jnp.zeros_like(state_ref)

    row = lax.broadcasted_iota(jnp.int32, (T, HDIM), 0)
    tt = lax.broadcasted_iota(jnp.int32, (T, T), 0)
    ss = lax.broadcasted_iota(jnp.int32, (T, T), 1)
    hs = lax.broadcasted_iota(jnp.int32, (H, T), 1)
    diag = tt == ss
    same2, same4, same8 = ((tt // n) == (ss // n) for n in (2, 4, 8))
    heads = range(NH)
    col = lambda hd: slice(hd * HDIM, (hd + 1) * HDIM)
    second_half = [mlow_ref[li] for li in range(len(HG_LOW))]
    first_half = [1.0 - mk for mk in second_half]
    q = [q_ref[:, col(hd)] for hd in heads]
    k = [k_ref[:, col(hd)] for hd in heads]
    v = [v_ref[:, col(hd)] for hd in heads]
    b = [b_ref[:, col(hd)] for hd in heads]

    inter = []
    for hd in heads:
        state = state_ref[hd]
        b_last = b[hd][T - 1:T, :]
        inter.append(_dot_nt(q[hd] * jnp.exp2(b[hd]).astype(BF16), state.astype(BF16)))
        k_out = k[hd] * jnp.exp2(b_last - b[hd]).astype(BF16)
        state_ref[hd] = state * jnp.exp2(b_last) + _dot_tn(v[hd], k_out)

    attn = []
    for hd in heads:
        bh = b[hd]
        prev1 = bh - pltpu.roll(bh, 1, 0)
        prev2 = bh - pltpu.roll(bh, 2, 0)
        next1 = pltpu.roll(bh, T - 1, 0) - bh
        b3 = bh.reshape(T // SUBLANES, SUBLANES, HDIM)
        mid = jnp.broadcast_to(b3[:, 3:4, :], b3.shape).reshape(T, HDIM)
        pos = row & 3
        exps = {
            4: jnp.where((row & 4) != 0, bh - mid, mid - bh),
            2: jnp.where(pos == 2, prev1, jnp.where(pos == 3, prev2,
                                                   jnp.where(pos == 0, next1, 0.0))),
            1: jnp.where((row & 1) != 0, prev1, 0.0),
        }
        prods = {}
        for li, m in enumerate(HG_LOW):
            dec = jnp.exp2(exps[m]).astype(BF16)
            q_m = q[hd] * (dec * second_half[li])
            k_m = k[hd] * (dec * first_half[li])
            prods[m] = _dot_nt(q_m, k_m)
        pd = _dot_nt(q[hd], k[hd])
        attn.append(jnp.where(same2, jnp.where(diag, pd, prods[1]),
                              jnp.where(same4, prods[2], jnp.where(same8, prods[4], 0.0))))

    for m in HG_HIGH:
        nb = T // (2 * m)
        split = lambda a: a.reshape(nb, 2 * m, HDIM)
        second = lambda a: split(a)[:, m:, :].reshape(H, HDIM)
        keep = ((lax.broadcasted_iota(jnp.int32, (H, T), 0) // m) == (hs // (2 * m))) & (
            (hs & m) == 0)
        for hd in heads:
            b3 = split(b[hd])
            mid = b3[:, m - 1:m, :]
            dec_b = jnp.exp2(b3[:, m:, :] - mid).reshape(H, HDIM)
            dec_a = jnp.concatenate(
                [jnp.exp2(mid - b3[:, :m, :]), jnp.zeros((nb, m, HDIM), F32)], axis=1
            ).reshape(T, HDIM)
            if m >= 2 * SUBLANES:
                q_b = second(q[hd]) * dec_b.astype(BF16)
            else:
                q_b = (second(q[hd].astype(F32)) * dec_b).astype(BF16)
            k_a = k[hd] * dec_a.astype(BF16)
            p = _dot_nt(q_b, k_a)
            a3 = attn[hd].reshape(nb, 2 * m, T)
            upd = jnp.where(keep, p, a3[:, m:, :].reshape(H, T)).reshape(nb, m, T)
            attn[hd] = jnp.concatenate([a3[:, :m, :], upd], axis=1).reshape(T, T)

    for hd in heads:
        o = inter[hd] + _dot(attn[hd].astype(BF16), v[hd])
        o = (_rms(o) * gain_ref[:, col(hd)]).astype(BF16) * gate_ref[:, col(hd)]
        o_ref[:, col(hd)] = o.astype(o_ref.dtype)


def _hgrn(proj, b, out_gain):
    S = proj.shape[0]
    T, hp = HG_BLOCK, HG_HEADS_PER_STEP
    w = hp * HDIM
    col = lambda off: pl.BlockSpec((T, w), lambda h, i: (i, off // hp + h))
    t = np.arange(T)
    mlow = jnp.asarray(np.stack([np.broadcast_to(((t & m) != 0)[:, None], (T, HDIM))
                                 for m in HG_LOW]).astype(np.float32), BF16)
    return pl.pallas_call(
        _hgrn_kernel,
        grid=(HEADS // hp, S // T),
        in_specs=[
            pl.BlockSpec(mlow.shape, lambda h, i: (0, 0, 0)),
            col(COL_Q_HG), col(COL_K_HG), col(COL_V_HG), col(0), col(COL_GATE),
            pl.BlockSpec((1, w), lambda h, i: (0, h)),
        ],
        out_specs=pl.BlockSpec((T, w), lambda h, i: (i, h)),
        out_shape=jax.ShapeDtypeStruct((S, GROUP_W), BF16),
        scratch_shapes=[pltpu.VMEM((hp, HDIM, HDIM), F32)],
        compiler_params=pltpu.CompilerParams(
            dimension_semantics=("arbitrary", "arbitrary"), vmem_limit_bytes=VMEM_LIMIT),
        name="hgrn2",
    )(mlow, proj, proj, proj, b, proj, out_gain)


def _bias_kernel(row_ref, o_ref):
    x = jnp.broadcast_to(row_ref[0], (ATT_SUB, BIAS_EXT))
    y = pltpu.roll(x, 0, 1, stride=1, stride_axis=0)[:, :ATT_KEYS]
    r = lax.broadcasted_iota(jnp.int32, (ATT_SUB, ATT_KEYS), 0)
    u = lax.broadcasted_iota(jnp.int32, (ATT_SUB, ATT_KEYS), 1)
    lo = jnp.where(r < CHUNK, 0, CHUNK)
    visible = (u >= lo) & (u < lo + BAND_CHUNKS * CHUNK)
    o_ref[0] = jnp.where(visible, y, -jnp.inf)


def _rel_bias(rel_bias):
    far = rel_bias[:, 2 * REL_CLIP:]
    pad = BAND_CHUNKS * CHUNK - REL_CLIP - CHUNK
    ext = jnp.concatenate([
        jnp.broadcast_to(far, (HEADS, pad)),
        rel_bias[:, :0:-1],
        jnp.broadcast_to(far, (HEADS, BIAS_EXT - pad - 2 * REL_CLIP)),
    ], axis=1).reshape(HEADS, 1, BIAS_EXT)
    return pl.pallas_call(
        _bias_kernel,
        grid=(HEADS,),
        in_specs=[pl.BlockSpec((1, 1, BIAS_EXT), lambda h: (h, 0, 0))],
        out_specs=pl.BlockSpec((1, ATT_SUB, ATT_KEYS), lambda h: (h, 0, 0)),
        out_shape=jax.ShapeDtypeStruct((HEADS, ATT_SUB, ATT_KEYS), F32),
        name="rel_bias",
    )(ext)


ATT_HEADS_PER_STEP = 4


def _attn_body(bias_ref, q_ref, kp_ref, kc_ref, vp_ref, vc_ref, o_ref, first):
    n_sub = ATT_BLOCK // ATT_SUB
    u = lax.broadcasted_iota(jnp.int32, (ATT_SUB, ATT_KEYS), 1)
    jobs = [(hd, j) for hd in range(ATT_HEADS_PER_STEP) for j in range(n_sub)]
    col = lambda hd: slice(hd * HDIM, (hd + 1) * HDIM)
    rows = lambda j: slice(j * ATT_SUB, (j + 1) * ATT_SUB)
    keys = lambda j: slice(j * ATT_SUB, j * ATT_SUB + ATT_KEYS)
    kcat = [jnp.concatenate([kp_ref[:, col(hd)], kc_ref[:, col(hd)]], axis=0)
            for hd in range(ATT_HEADS_PER_STEP)]
    vcat = [jnp.concatenate([vp_ref[:, col(hd)], vc_ref[:, col(hd)]], axis=0)
            for hd in range(ATT_HEADS_PER_STEP)]

    scores = []
    for hd, j in jobs:
        s = _dot_nt(q_ref[rows(j), col(hd)], kcat[hd][keys(j)]) + bias_ref[hd]
        if first:
            s = jnp.where(u < ATT_BLOCK - j * ATT_SUB, -jnp.inf, s)
        scores.append(s)
    probs, sums = [], []
    for s in scores:
        p = jnp.exp(s - jnp.max(s, axis=-1, keepdims=True))
        sums.append(jnp.sum(p, axis=-1, keepdims=True))
        probs.append(p.astype(BF16))
    for (hd, j), p, l in zip(jobs, probs, sums):
        o = _dot(p, vcat[hd][keys(j)]) / l
        o_ref[rows(j), col(hd)] = o.astype(o_ref.dtype)


def _attn_kernel(*refs):
    first = pl.program_id(1) == 0

    @pl.when(first)
    def _():
        _attn_body(*refs, first=True)

    @pl.when(jnp.logical_not(first))
    def _():
        _attn_body(*refs, first=False)


def _attention(proj, bias):
    S = proj.shape[0]
    hp = ATT_HEADS_PER_STEP
    w = hp * HDIM
    cur = lambda off: pl.BlockSpec((ATT_BLOCK, w), lambda h, i: (i, off // hp + h))
    prev = lambda off: pl.BlockSpec(
        (ATT_BLOCK, w), lambda h, i: (jnp.maximum(i - 1, 0), off // hp + h))
    return pl.pallas_call(
        _attn_kernel,
        grid=(HEADS // hp, S // ATT_BLOCK),
        in_specs=[
            pl.BlockSpec((hp, ATT_SUB, ATT_KEYS), lambda h, i: (h, 0, 0)),
            cur(COL_Q_AT), prev(COL_K_AT), cur(COL_K_AT), prev(COL_V_AT), cur(COL_V_AT),
        ],
        out_specs=pl.BlockSpec((ATT_BLOCK, w), lambda h, i: (i, h)),
        out_shape=jax.ShapeDtypeStruct((S, GROUP_W), BF16),
        compiler_params=pltpu.CompilerParams(
            dimension_semantics=("arbitrary", "arbitrary"), vmem_limit_bytes=VMEM_LIMIT),
        name="band_attn",
    )(bias, proj, proj, proj, proj, proj)


def _outproj_kernel(ohg_ref, oat_ref, x_ref, w_hbm, again_ref, gffn_ref, x1_ref, h2_ref,
                    w_ref, stage_ref, sem_ref, *, sub):
    @pl.when(pl.program_id(0) == 0)
    def _():
        _load_weight(w_hbm, 0, w_ref, stage_ref, sem_ref)

    for r0 in range(0, x_ref.shape[0], sub):
        rows = slice(r0, r0 + sub)
        oat = (_rms(oat_ref[rows, :].astype(F32)) * again_ref[...]).astype(BF16)
        x1 = (x_ref[rows, :] + _dot(ohg_ref[rows, :], w_ref[0:GROUP_W, :])
              + _dot(oat, w_ref[GROUP_W:, :]))
        x1_ref[rows, :] = x1
        h2_ref[rows, :] = (_rms(x1) * gffn_ref[...]).astype(BF16)


def _outproj(o_hg, o_at, x2, w_out, at_gain, g_ffn, tm):
    S, D = x2.shape
    full = lambda shape: pl.BlockSpec(shape, lambda i: (0, 0))
    rows = lambda w: pl.BlockSpec((tm, w), lambda i: (i, 0))
    return pl.pallas_call(
        functools.partial(_outproj_kernel, sub=tm // 2),
        grid=(S // tm,),
        in_specs=[rows(GROUP_W), rows(GROUP_W), rows(D),
                  pl.BlockSpec(memory_space=pl.ANY),
                  full((1, GROUP_W)), full((1, D))],
        out_specs=[rows(D), rows(D)],
        out_shape=[jax.ShapeDtypeStruct((S, D), F32), jax.ShapeDtypeStruct((S, D), BF16)],
        scratch_shapes=_weight_scratch(*w_out.shape),
        compiler_params=pltpu.CompilerParams(
            dimension_semantics=("arbitrary",), vmem_limit_bytes=VMEM_LIMIT),
        name="outproj",
    )(o_hg, o_at, x2, w_out, at_gain, g_ffn)


def _conv3(u, cw, cb):
    z = pltpu.roll(cw[0:1, :] * u, 1, 0) + cw[1:2, :] * u
    return pltpu.roll(z, 1, 0) + (cw[2:3, :] * u + cb)


def _ffn_up_kernel(h_ref, w_hbm, cwa_ref, cwg_ref, cba_ref, cbg_ref, o_ref,
                   ta_ref, tg_ref, wa_ref, stage_a, sem_a, wg_ref, stage_g, sem_g, *, cblk):
    tm, tn = o_ref.shape
    pad = SUBLANES

    @pl.when(pl.program_id(1) == 0)
    def _():
        ta_ref[...] = jnp.zeros_like(ta_ref)
        tg_ref[...] = jnp.zeros_like(tg_ref)
        slab = pl.program_id(0)
        d_ff = w_hbm.shape[1] // 2
        _load_weight(w_hbm, pl.multiple_of(slab * tn, HDIM), wa_ref, stage_a, sem_a)
        _load_weight(w_hbm, pl.multiple_of(d_ff + slab * tn, HDIM), wg_ref, stage_g, sem_g)

    def conv(u, t_ref, cw_ref, cb_ref, cols, scale):
        cw = cw_ref[:, cols] * scale
        cb = cb_ref[:, cols] * scale
        y = _conv3(u, cw, cb)
        top = _conv3(jnp.concatenate([t_ref[:, cols], u[0:pad]], axis=0), cw, cb)
        t_ref[:, cols] = u[tm - pad:tm]
        return jnp.concatenate([top[pad:2 * pad], y[pad:]], axis=0)

    h = h_ref[...]
    for c0 in range(0, tn, cblk):
        cols = slice(c0, c0 + cblk)
        ya = conv(_dot(h, wa_ref[:, cols]), ta_ref, cwa_ref, cba_ref, cols, 1.0)
        yh = conv(_dot(h, wg_ref[:, cols]), tg_ref, cwg_ref, cbg_ref, cols, 0.5)
        o_ref[:, cols] = (ya * (yh * (1.0 + jnp.tanh(yh)))).astype(o_ref.dtype)


def _ffn_up(h2, w_up, conv_w, conv_b, tm, n_slabs, cblk):
    S, D = h2.shape
    d_ff = w_up.shape[1] // 2
    tn = d_ff // n_slabs
    a_col = lambda r: pl.BlockSpec((r, tn), lambda c, m: (0, c))
    g_col = lambda r: pl.BlockSpec((r, tn), lambda c, m: (0, n_slabs + c))
    return pl.pallas_call(
        functools.partial(_ffn_up_kernel, cblk=cblk),
        grid=(n_slabs, S // tm),
        in_specs=[pl.BlockSpec((tm, D), lambda c, m: (m, 0)),
                  pl.BlockSpec(memory_space=pl.ANY),
                  a_col(CONV_W), g_col(CONV_W), a_col(1), g_col(1)],
        out_specs=pl.BlockSpec((tm, tn), lambda c, m: (m, c)),
        out_shape=jax.ShapeDtypeStruct((S, d_ff), BF16),
        scratch_shapes=[pltpu.VMEM((SUBLANES, tn), F32)] * 2 + 2 * _weight_scratch(D, tn),
        compiler_params=pltpu.CompilerParams(
            dimension_semantics=("arbitrary", "arbitrary"), vmem_limit_bytes=VMEM_LIMIT),
        name="ffn_up",
    )(h2, w_up, conv_w, conv_w, conv_b, conv_b)


def _ffn_down_kernel(a_ref, w_hbm, x1_ref, o_ref, w_ref, stage_ref, sem_ref):
    @pl.when(pl.program_id(0) == 0)
    def _():
        _load_weight(w_hbm, 0, w_ref, stage_ref, sem_ref)

    o_ref[...] = x1_ref[...] + _dot(a_ref[...], w_ref[...])


def _ffn_down(act, w_down, x1, tm):
    S, D = x1.shape
    d_ff = act.shape[1]
    return pl.pallas_call(
        _ffn_down_kernel,
        grid=(S // tm,),
        in_specs=[pl.BlockSpec((tm, d_ff), lambda i: (i, 0)),
                  pl.BlockSpec(memory_space=pl.ANY),
                  pl.BlockSpec((tm, D), lambda i: (i, 0))],
        out_specs=pl.BlockSpec((tm, D), lambda i: (i, 0)),
        out_shape=jax.ShapeDtypeStruct((S, D), F32),
        scratch_shapes=_weight_scratch(d_ff, D),
        compiler_params=pltpu.CompilerParams(
            dimension_semantics=("arbitrary",), vmem_limit_bytes=VMEM_LIMIT),
        name="ffn_down",
    )(act, w_down, x1)


def kernel(x, g_mix, w_in, hg_lb, hg_out_gain, q_gain, k_gain, rel_bias, at_out_gain, w_out,
           g_ffn, w_up, conv_w, conv_b, w_down):
    B, S, D = x.shape
    depth = g_mix.shape[0]
    assert B == 1 and depth == 1 and S % 1024 == 0
    assert w_in.shape[2] == N_GROUPS * GROUP_W
    x2 = x.reshape(S, D)
    proj, b = _inproj(x2, g_mix, w_in[0], hg_lb, q_gain, k_gain, tm=HG_BLOCK)
    o_hg = _hgrn(proj, b, hg_out_gain)
    o_at = _attention(proj, _rel_bias(rel_bias[0]))
    x1, h2 = _outproj(o_hg, o_at, x2, w_out[0], at_out_gain, g_ffn, tm=512)
    act = _ffn_up(h2, w_up[0], conv_w[0], conv_b, tm=256, n_slabs=2, cblk=256)
    out = _ffn_down(act, w_down[0], x1, tm=512)
    return out.reshape(B, S, D)
```

```python
import functools

import jax
import jax.numpy as jnp
import numpy as np
from jax import lax
from jax.experimental import pallas as pl
from jax.experimental.pallas import tpu as pltpu

F32 = jnp.float32
BF16 = jnp.bfloat16

CHUNK = 64
EPS = 1e-6
HEADS = 8
HDIM = 128
GROUP_W = HEADS * HDIM
N_GROUPS = 7
BAND_CHUNKS = 9
REL_CLIP = 128
CONV_W = 3
SUBLANES = 8
LOG2E = 1.4426950408889634

COL_Q_HG, COL_K_HG, COL_V_HG, COL_GATE, COL_Q_AT, COL_K_AT, COL_V_AT = (
    0, 8, 16, 24, 32, 40, 48)

ATT_BLOCK = 512
ATT_SUB = 128
ATT_KEYS = ATT_SUB + (BAND_CHUNKS - 1) * CHUNK
BIAS_EXT = 768

VMEM_LIMIT = 56 * 1024 * 1024


def _dot(a, b):
    return jnp.dot(a, b, preferred_element_type=F32)


def _dot_nt(a, b):
    return lax.dot_general(a, b, (((1,), (1,)), ((), ())), preferred_element_type=F32)


def _dot_tn(a, b):
    return lax.dot_general(a, b, (((0,), (0,)), ((), ())), preferred_element_type=F32)


def _sigmoid(x):
    return 1.0 / (1.0 + jnp.exp(-x))


def _tril(n):
    return jnp.asarray(np.tril(np.ones((n, n), np.float32)), BF16)


def _rms(x):
    return x * lax.rsqrt(jnp.mean(x * x, axis=-1, keepdims=True) + EPS)


WEIGHT_STAGE_BYTES = 2 * 1024 * 1024
WEIGHT_PANEL = 1024


def _stage_shape(n_rows, n_cols):
    panel = WEIGHT_PANEL
    while n_cols % panel:
        panel -= HDIM
    rows = max(SUBLANES, WEIGHT_STAGE_BYTES // (4 * panel))
    while n_rows % rows:
        rows //= 2
    return rows, panel


def _weight_scratch(n_rows, n_cols):
    return [pltpu.VMEM((n_rows, n_cols), BF16),
            pltpu.VMEM((2,) + _stage_shape(n_rows, n_cols), F32),
            pltpu.SemaphoreType.DMA((2,))]


def _load_weight(w_hbm, col0, w_ref, stage_ref, sem_ref):
    n_rows, n_cols = w_ref.shape
    _, rows, panel = stage_ref.shape
    chunks = [(r0, c0) for c0 in range(0, n_cols, panel) for r0 in range(0, n_rows, rows)]

    def copy(i):
        r0, c0 = chunks[i]
        return pltpu.make_async_copy(
            w_hbm.at[pl.ds(r0, rows), pl.ds(col0 + c0, panel)],
            stage_ref.at[i % 2], sem_ref.at[i % 2])

    copy(0).start()
    for i, (r0, c0) in enumerate(chunks):
        if i + 1 < len(chunks):
            copy(i + 1).start()
        copy(i).wait()
        w_ref[r0:r0 + rows, c0:c0 + panel] = stage_ref[i % 2].astype(BF16)


def _inproj_kernel(x_ref, gmix_ref, w_hbm, lbp_ref, qg_ref, kg_ref, tri_ref, out_ref, b_ref,
                   w_ref, stage_ref, sem_ref):
    @pl.when(pl.program_id(0) == 0)
    def _():
        _load_weight(w_hbm, 0, w_ref, stage_ref, sem_ref)

    h = (_rms(x_ref[...]) * gmix_ref[...]).astype(BF16)

    def proj(j):
        return _dot(h, w_ref[:, j * GROUP_W:(j + 1) * GROUP_W])

    def put(j, val):
        out_ref[:, j * GROUP_W:(j + 1) * GROUP_W] = val.astype(BF16)

    a = proj(0)
    put(0, a * _sigmoid(a))

    a = proj(1)
    p0 = lbp_ref[0:1, :]
    p1 = lbp_ref[1:2, :]
    pm = jnp.maximum(p0, p1)
    e0 = jnp.exp(p0 - pm)
    e1 = jnp.exp(p1 - pm)
    lb = e0 / (e0 + e1)
    s = _sigmoid(a)
    put(1, (1.0 - lb) * (1.0 - s))
    g = jnp.log2(lb + (1.0 - lb) * s)
    g_hi = g.astype(BF16)
    g_lo = (g - g_hi.astype(F32)).astype(BF16)
    b_ref[...] = _dot(tri_ref[...], g_hi) + _dot(tri_ref[...], g_lo)

    put(2, proj(2))

    a = proj(3)
    put(3, a * _sigmoid(a))

    scale = HDIM ** -0.5 * LOG2E
    for j, gain_ref, mul in ((4, qg_ref, scale), (5, kg_ref, 1.0)):
        a = proj(j)
        gain = gain_ref[...] * mul
        for hd in range(HEADS):
            ah = a[:, hd * HDIM:(hd + 1) * HDIM]
            c0 = j * GROUP_W + hd * HDIM
            out_ref[:, c0:c0 + HDIM] = (_rms(ah) * gain).astype(BF16)

    put(6, proj(6))


def _inproj(x2, g_mix, w_in, hg_lb, q_gain, k_gain, tm):
    S, D = x2.shape
    n_out = w_in.shape[1]
    full = lambda shape: pl.BlockSpec(shape, lambda i: (0, 0))
    return pl.pallas_call(
        _inproj_kernel,
        grid=(S // tm,),
        in_specs=[
            pl.BlockSpec((tm, D), lambda i: (i, 0)),
            full((1, D)),
            pl.BlockSpec(memory_space=pl.ANY),
            full(hg_lb.shape),
            full((1, HDIM)),
            full((1, HDIM)),
            full((tm, tm)),
        ],
        out_specs=[
            pl.BlockSpec((tm, n_out), lambda i: (i, 0)),
            pl.BlockSpec((tm, GROUP_W), lambda i: (i, 0)),
        ],
        out_shape=[
            jax.ShapeDtypeStruct((S, n_out), BF16),
            jax.ShapeDtypeStruct((S, GROUP_W), F32),
        ],
        scratch_shapes=_weight_scratch(D, n_out),
        compiler_params=pltpu.CompilerParams(
            dimension_semantics=("arbitrary",), vmem_limit_bytes=VMEM_LIMIT),
        name="inproj",
    )(x2, g_mix, w_in, hg_lb, q_gain, k_gain, _tril(tm))


HG_BLOCK = 256
HG_HEADS_PER_STEP = 8
HG_HIGH = (128, 64, 32, 16, 8)
HG_LOW = (4, 2, 1)


def _hgrn_kernel(mlow_ref, q_ref, k_ref, v_ref, b_ref, gate_ref, gain_ref, o_ref, state_ref):
    T = HG_BLOCK
    H = T // 2
    NH = HG_HEADS_PER_STEP

    @pl.when(pl.program_id(1) == 0)
    def _():
        state_ref[...] = jnp.zeros_like(state_ref)

    row = lax.broadcasted_iota(jnp.int32, (T, HDIM), 0)
    tt = lax.broadcasted_iota(jnp.int32, (T, T), 0)
    ss = lax.broadcasted_iota(jnp.int32, (T, T), 1)
    hs = lax.broadcasted_iota(jnp.int32, (H, T), 1)
    diag = tt == ss
    same2, same4, same8 = ((tt // n) == (ss // n) for n in (2, 4, 8))
    heads = range(NH)
    col = lambda hd: slice(hd * HDIM, (hd + 1) * HDIM)
    second_half = [mlow_ref[li] for li in range(len(HG_LOW))]
    first_half = [1.0 - mk for mk in second_half]
    q = [q_ref[:, col(hd)] for hd in heads]
    k = [k_ref[:, col(hd)] for hd in heads]
    v = [v_ref[:, col(hd)] for hd in heads]
    b = [b_ref[:, col(hd)] for hd in heads]

    inter = []
    for hd in heads:
        state = state_ref[hd]
        b_last = b[hd][T - 1:T, :]
        inter.append(_dot_nt(q[hd] * jnp.exp2(b[hd]).astype(BF16), state.astype(BF16)))
        k_out = k[hd] * jnp.exp2(b_last - b[hd]).astype(BF16)
        state_ref[hd] = state * jnp.exp2(b_last) + _dot_tn(v[hd], k_out)

    attn = []
    for hd in heads:
        bh = b[hd]
        prev1 = bh - pltpu.roll(bh, 1, 0)
        prev2 = bh - pltpu.roll(bh, 2, 0)
        next1 = pltpu.roll(bh, T - 1, 0) - bh
        b3 = bh.reshape(T // SUBLANES, SUBLANES, HDIM)
        mid = jnp.broadcast_to(b3[:, 3:4, :], b3.shape).reshape(T, HDIM)
        pos = row & 3
        exps = {
            4: jnp.where((row & 4) != 0, bh - mid, mid - bh),
            2: jnp.where(pos == 2, prev1, jnp.where(pos == 3, prev2,
                                                   jnp.where(pos == 0, next1, 0.0))),
            1: jnp.where((row & 1) != 0, prev1, 0.0),
        }
        prods = {}
        for li, m in enumerate(HG_LOW):
            dec = jnp.exp2(exps[m]).astype(BF16)
            q_m = q[hd] * (dec * second_half[li])
            k_m = k[hd] * (dec * first_half[li])
            prods[m] = _dot_nt(q_m, k_m)
        pd = _dot_nt(q[hd], k[hd])
        attn.append(jnp.where(same2, jnp.where(diag, pd, prods[1]),
                              jnp.where(same4, prods[2], jnp.where(same8, prods[4], 0.0))))

    for m in HG_HIGH:
        nb = T // (2 * m)
        split = lambda a: a.reshape(nb, 2 * m, HDIM)
        second = lambda a: split(a)[:, m:, :].reshape(H, HDIM)
        keep = ((lax.broadcasted_iota(jnp.int32, (H, T), 0) // m) == (hs // (2 * m))) & (
            (hs & m) == 0)
        for hd in heads:
            b3 = split(b[hd])
            mid = b3[:, m - 1:m, :]
            dec_b = jnp.exp2(b3[:, m:, :] - mid).reshape(H, HDIM)
            dec_a = jnp.exp2(mid - b3[:, :m, :])
            if m >= 2 * SUBLANES:
                q_b = second(q[hd]) * dec_b.astype(BF16)
                k_a = split(k[hd])[:, :m, :] * dec_a.astype(BF16)
            else:
                q_b = (second(q[hd].astype(F32)) * dec_b).astype(BF16)
                k_a = (split(k[hd].astype(F32))[:, :m, :] * dec_a).astype(BF16)
            k_a = jnp.concatenate([k_a, jnp.zeros_like(k_a)], axis=1).reshape(T, HDIM)
            p = _dot_nt(q_b, k_a)
            a3 = attn[hd].reshape(nb, 2 * m, T)
            upd = jnp.where(keep, p, a3[:, m:, :].reshape(H, T)).reshape(nb, m, T)
            attn[hd] = jnp.concatenate([a3[:, :m, :], upd], axis=1).reshape(T, T)

    for hd in heads:
        o = inter[hd] + _dot(attn[hd].astype(BF16), v[hd])
        o = (_rms(o) * gain_ref[:, col(hd)]).astype(BF16) * gate_ref[:, col(hd)]
        o_ref[:, col(hd)] = o.astype(o_ref.dtype)


def _hgrn(proj, b, out_gain):
    S = proj.shape[0]
    T, hp = HG_BLOCK, HG_HEADS_PER_STEP
    w = hp * HDIM
    col = lambda off: pl.BlockSpec((T, w), lambda h, i: (i, off // hp + h))
    t = np.arange(T)
    mlow = jnp.asarray(np.stack([np.broadcast_to(((t & m) != 0)[:, None], (T, HDIM))
                                 for m in HG_LOW]).astype(np.float32), BF16)
    return pl.pallas_call(
        _hgrn_kernel,
        grid=(HEADS // hp, S // T),
        in_specs=[
            pl.BlockSpec(mlow.shape, lambda h, i: (0, 0, 0)),
            col(COL_Q_HG), col(COL_K_HG), col(COL_V_HG), col(0), col(COL_GATE),
            pl.BlockSpec((1, w), lambda h, i: (0, h)),
        ],
        out_specs=pl.BlockSpec((T, w), lambda h, i: (i, h)),
        out_shape=jax.ShapeDtypeStruct((S, GROUP_W), BF16),
        scratch_shapes=[pltpu.VMEM((hp, HDIM, HDIM), F32)],
        compiler_params=pltpu.CompilerParams(
            dimension_semantics=("arbitrary", "arbitrary"), vmem_limit_bytes=VMEM_LIMIT),
        name="hgrn2",
    )(mlow, proj, proj, proj, b, proj, out_gain)


def _bias_kernel(row_ref, o_ref):
    x = jnp.broadcast_to(row_ref[0], (ATT_SUB, BIAS_EXT))
    y = pltpu.roll(x, 0, 1, stride=1, stride_axis=0)[:, :ATT_KEYS]
    r = lax.broadcasted_iota(jnp.int32, (ATT_SUB, ATT_KEYS), 0)
    u = lax.broadcasted_iota(jnp.int32, (ATT_SUB, ATT_KEYS), 1)
    lo = jnp.where(r < CHUNK, 0, CHUNK)
    visible = (u >= lo) & (u < lo + BAND_CHUNKS * CHUNK)
    o_ref[0] = jnp.where(visible, y * LOG2E, -jnp.inf)


def _rel_bias(rel_bias):
    far = rel_bias[:, 2 * REL_CLIP:]
    pad = BAND_CHUNKS * CHUNK - REL_CLIP - CHUNK
    ext = jnp.concatenate([
        jnp.broadcast_to(far, (HEADS, pad)),
        rel_bias[:, :0:-1],
        jnp.broadcast_to(far, (HEADS, BIAS_EXT - pad - 2 * REL_CLIP)),
    ], axis=1).reshape(HEADS, 1, BIAS_EXT)
    return pl.pallas_call(
        _bias_kernel,
        grid=(HEADS,),
        in_specs=[pl.BlockSpec((1, 1, BIAS_EXT), lambda h: (h, 0, 0))],
        out_specs=pl.BlockSpec((1, ATT_SUB, ATT_KEYS), lambda h: (h, 0, 0)),
        out_shape=jax.ShapeDtypeStruct((HEADS, ATT_SUB, ATT_KEYS), F32),
        name="rel_bias",
    )(ext)


ATT_HEADS_PER_STEP = 4
ATT_WAVE = 2


def _attn_body(bias_ref, q_ref, kp_ref, kc_ref, vp_ref, vc_ref, o_ref, first):
    n_sub = ATT_BLOCK // ATT_SUB
    u = lax.broadcasted_iota(jnp.int32, (ATT_SUB, ATT_KEYS), 1)
    jobs = [(hd, j) for hd in range(ATT_HEADS_PER_STEP) for j in range(n_sub)]
    col = lambda hd: slice(hd * HDIM, (hd + 1) * HDIM)
    rows = lambda j: slice(j * ATT_SUB, (j + 1) * ATT_SUB)
    keys = lambda j: slice(j * ATT_SUB, j * ATT_SUB + ATT_KEYS)
    kcat = [jnp.concatenate([kp_ref[:, col(hd)], kc_ref[:, col(hd)]], axis=0)
            for hd in range(ATT_HEADS_PER_STEP)]
    vcat = [jnp.concatenate([vp_ref[:, col(hd)], vc_ref[:, col(hd)]], axis=0)
            for hd in range(ATT_HEADS_PER_STEP)]

    def qk(hd, j):
        s = _dot_nt(q_ref[rows(j), col(hd)], kcat[hd][keys(j)]) + bias_ref[hd]
        if first:
            s = jnp.where(u < ATT_BLOCK - j * ATT_SUB, -jnp.inf, s)
        return s

    def softmax(s):
        p = jnp.exp2(s - jnp.max(s, axis=-1, keepdims=True))
        return p.astype(BF16), jnp.sum(p, axis=-1, keepdims=True)

    def pv(hd, j, p, l):
        o = _dot(p, vcat[hd][keys(j)]) / l
        o_ref[rows(j), col(hd)] = o.astype(o_ref.dtype)

    waves = [jobs[i:i + ATT_WAVE] for i in range(0, len(jobs), ATT_WAVE)]
    scores = [qk(hd, j) for hd, j in waves[0]]
    for w, wave in enumerate(waves):
        nxt = [qk(hd, j) for hd, j in waves[w + 1]] if w + 1 < len(waves) else None
        for (hd, j), s in zip(wave, scores):
            pv(hd, j, *softmax(s))
        scores = nxt


def _attn_kernel(*refs):
    first = pl.program_id(1) == 0

    @pl.when(first)
    def _():
        _attn_body(*refs, first=True)

    @pl.when(jnp.logical_not(first))
    def _():
        _attn_body(*refs, first=False)


def _attention(proj, bias):
    S = proj.shape[0]
    hp = ATT_HEADS_PER_STEP
    w = hp * HDIM
    cur = lambda off: pl.BlockSpec((ATT_BLOCK, w), lambda h, i: (i, off // hp + h))
    prev = lambda off: pl.BlockSpec(
        (ATT_BLOCK, w), lambda h, i: (jnp.maximum(i - 1, 0), off // hp + h))
    return pl.pallas_call(
        _attn_kernel,
        grid=(HEADS // hp, S // ATT_BLOCK),
        in_specs=[
            pl.BlockSpec((hp, ATT_SUB, ATT_KEYS), lambda h, i: (h, 0, 0)),
            cur(COL_Q_AT), prev(COL_K_AT), cur(COL_K_AT), prev(COL_V_AT), cur(COL_V_AT),
        ],
        out_specs=pl.BlockSpec((ATT_BLOCK, w), lambda h, i: (i, h)),
        out_shape=jax.ShapeDtypeStruct((S, GROUP_W), BF16),
        compiler_params=pltpu.CompilerParams(
            dimension_semantics=("arbitrary", "arbitrary"), vmem_limit_bytes=VMEM_LIMIT),
        name="band_attn",
    )(bias, proj, proj, proj, proj, proj)


def _outproj_kernel(ohg_ref, oat_ref, x_ref, w_hbm, again_ref, gffn_ref, x1_ref, h2_ref,
                    w_ref, stage_ref, sem_ref, *, sub):
    @pl.when(pl.program_id(0) == 0)
    def _():
        _load_weight(w_hbm, 0, w_ref, stage_ref, sem_ref)

    for r0 in range(0, x_ref.shape[0], sub):
        rows = slice(r0, r0 + sub)
        oat = (_rms(oat_ref[rows, :].astype(F32)) * again_ref[...]).astype(BF16)
        x1 = (x_ref[rows, :] + _dot(ohg_ref[rows, :], w_ref[0:GROUP_W, :])
              + _dot(oat, w_ref[GROUP_W:, :]))
        x1_ref[rows, :] = x1
        h2_ref[rows, :] = (_rms(x1) * gffn_ref[...]).astype(BF16)


def _outproj(o_hg, o_at, x2, w_out, at_gain, g_ffn, tm):
    S, D = x2.shape
    full = lambda shape: pl.BlockSpec(shape, lambda i: (0, 0))
    rows = lambda w: pl.BlockSpec((tm, w), lambda i: (i, 0))
    return pl.pallas_call(
        functools.partial(_outproj_kernel, sub=tm // 2),
        grid=(S // tm,),
        in_specs=[rows(GROUP_W), rows(GROUP_W), rows(D),
                  pl.BlockSpec(memory_space=pl.ANY),
                  full((1, GROUP_W)), full((1, D))],
        out_specs=[rows(D), rows(D)],
        out_shape=[jax.ShapeDtypeStruct((S, D), F32), jax.ShapeDtypeStruct((S, D), BF16)],
        scratch_shapes=_weight_scratch(*w_out.shape),
        compiler_params=pltpu.CompilerParams(
            dimension_semantics=("arbitrary",), vmem_limit_bytes=VMEM_LIMIT),
        name="outproj",
    )(o_hg, o_at, x2, w_out, at_gain, g_ffn)


def _conv3(u, cw, cb):
    z = pltpu.roll(cw[0:1, :] * u, 1, 0) + cw[1:2, :] * u
    return pltpu.roll(z, 1, 0) + (cw[2:3, :] * u + cb)


def _ffn_up_kernel(h_ref, w_hbm, cwa_ref, cwg_ref, cba_ref, cbg_ref, o_ref,
                   ta_ref, tg_ref, wa_ref, stage_a, sem_a, wg_ref, stage_g, sem_g, *, cblk):
    tm, tn = o_ref.shape
    pad = SUBLANES

    @pl.when(pl.program_id(1) == 0)
    def _():
        ta_ref[...] = jnp.zeros_like(ta_ref)
        tg_ref[...] = jnp.zeros_like(tg_ref)
        slab = pl.program_id(0)
        d_ff = w_hbm.shape[1] // 2
        _load_weight(w_hbm, pl.multiple_of(slab * tn, HDIM), wa_ref, stage_a, sem_a)
        _load_weight(w_hbm, pl.multiple_of(d_ff + slab * tn, HDIM), wg_ref, stage_g, sem_g)

    def conv(u, t_ref, cw_ref, cb_ref, cols, scale):
        cw = cw_ref[:, cols] * scale
        cb = cb_ref[:, cols] * scale
        y = _conv3(u, cw, cb)
        top = _conv3(jnp.concatenate([t_ref[:, cols], u[0:pad]], axis=0), cw, cb)
        t_ref[:, cols] = u[tm - pad:tm]
        return jnp.concatenate([top[pad:2 * pad], y[pad:]], axis=0)

    h = h_ref[...]
    for c0 in range(0, tn, cblk):
        cols = slice(c0, c0 + cblk)
        ya = conv(_dot(h, wa_ref[:, cols]), ta_ref, cwa_ref, cba_ref, cols, 1.0)
        yh = conv(_dot(h, wg_ref[:, cols]), tg_ref, cwg_ref, cbg_ref, cols, 0.5)
        o_ref[:, cols] = (ya * (yh * (1.0 + jnp.tanh(yh)))).astype(o_ref.dtype)


def _ffn_up(h2, w_up, conv_w, conv_b, tm, n_slabs, cblk):
    S, D = h2.shape
    d_ff = w_up.shape[1] // 2
    tn = d_ff // n_slabs
    a_col = lambda r: pl.BlockSpec((r, tn), lambda c, m: (0, c))
    g_col = lambda r: pl.BlockSpec((r, tn), lambda c, m: (0, n_slabs + c))
    return pl.pallas_call(
        functools.partial(_ffn_up_kernel, cblk=cblk),
        grid=(n_slabs, S // tm),
        in_specs=[pl.BlockSpec((tm, D), lambda c, m: (m, 0)),
                  pl.BlockSpec(memory_space=pl.ANY),
                  a_col(CONV_W), g_col(CONV_W), a_col(1), g_col(1)],
        out_specs=pl.BlockSpec((tm, tn), lambda c, m: (m, c)),
        out_shape=jax.ShapeDtypeStruct((S, d_ff), BF16),
        scratch_shapes=[pltpu.VMEM((SUBLANES, tn), F32)] * 2 + 2 * _weight_scratch(D, tn),
        compiler_params=pltpu.CompilerParams(
            dimension_semantics=("arbitrary", "arbitrary"), vmem_limit_bytes=VMEM_LIMIT),
        name="ffn_up",
    )(h2, w_up, conv_w, conv_w, conv_b, conv_b)


def _ffn_down_kernel(a_ref, w_hbm, x1_ref, o_ref, w_ref, stage_ref, sem_ref):
    @pl.when(pl.program_id(0) == 0)
    def _():
        _load_weight(w_hbm, 0, w_ref, stage_ref, sem_ref)

    o_ref[...] = x1_ref[...] + _dot(a_ref[...], w_ref[...])


def _ffn_down(act, w_down, x1, tm):
    S, D = x1.shape
    d_ff = act.shape[1]
    return pl.pallas_call(
        _ffn_down_kernel,
        grid=(S // tm,),
        in_specs=[pl.BlockSpec((tm, d_ff), lambda i: (i, 0)),
                  pl.BlockSpec(memory_space=pl.ANY),
                  pl.BlockSpec((tm, D), lambda i: (i, 0))],
        out_specs=pl.BlockSpec((tm, D), lambda i: (i, 0)),
        out_shape=jax.ShapeDtypeStruct((S, D), F32),
        scratch_shapes=_weight_scratch(d_ff, D),
        compiler_params=pltpu.CompilerParams(
            dimension_semantics=("arbitrary",), vmem_limit_bytes=VMEM_LIMIT),
        name="ffn_down",
    )(act, w_down, x1)


def kernel(x, g_mix, w_in, hg_lb, hg_out_gain, q_gain, k_gain, rel_bias, at_out_gain, w_out,
           g_ffn, w_up, conv_w, conv_b, w_down):
    B, S, D = x.shape
    depth = g_mix.shape[0]
    assert B == 1 and depth == 1 and S % 1024 == 0
    assert w_in.shape[2] == N_GROUPS * GROUP_W
    x2 = x.reshape(S, D)
    proj, b = _inproj(x2, g_mix, w_in[0], hg_lb, q_gain, k_gain, tm=HG_BLOCK)
    o_hg = _hgrn(proj, b, hg_out_gain)
    o_at = _attention(proj, _rel_bias(rel_bias[0]))
    x1, h2 = _outproj(o_hg, o_at, x2, w_out[0], at_out_gain, g_ffn, tm=512)
    act = _ffn_up(h2, w_up[0], conv_w[0], conv_b, tm=256, n_slabs=2, cblk=256)
    out = _ffn_down(act, w_down[0], x1, tm=512)
    return out.reshape(B, S, D)
```

```python
import functools

import jax
import jax.numpy as jnp
import numpy as np
from jax import lax
from jax.experimental import pallas as pl
from jax.experimental.pallas import tpu as pltpu

F32 = jnp.float32
BF16 = jnp.bfloat16

CHUNK = 64
EPS = 1e-6
HEADS = 8
HDIM = 128
GROUP_W = HEADS * HDIM
N_GROUPS = 7
BAND_CHUNKS = 9
REL_CLIP = 128
CONV_W = 3
SUBLANES = 8
LOG2E = 1.4426950408889634
PERM_TILE = 256
PERM_R = PERM_TILE // SUBLANES

COL_Q_HG, COL_K_HG, COL_V_HG, COL_GATE, COL_Q_AT, COL_K_AT, COL_V_AT = (
    0, 8, 16, 24, 32, 40, 48)

ATT_BLOCK = 512
ATT_SUB = 128
ATT_KEYS = ATT_SUB + (BAND_CHUNKS - 1) * CHUNK
BIAS_EXT = 768

VMEM_LIMIT = 56 * 1024 * 1024


def _dot(a, b):
    return jnp.dot(a, b, preferred_element_type=F32)


def _dot_nt(a, b):
    return lax.dot_general(a, b, (((1,), (1,)), ((), ())), preferred_element_type=F32)


def _dot_tn(a, b):
    return lax.dot_general(a, b, (((0,), (0,)), ((), ())), preferred_element_type=F32)


def _sigmoid(x):
    return 1.0 / (1.0 + jnp.exp(-x))


def _tril(n):
    return jnp.asarray(np.tril(np.ones((n, n), np.float32)), BF16)


def _rms(x):
    return x * lax.rsqrt(jnp.mean(x * x, axis=-1, keepdims=True) + EPS)


WEIGHT_STAGE_BYTES = 2 * 1024 * 1024
WEIGHT_PANEL = 1024


def _stage_shape(n_rows, n_cols):
    panel = WEIGHT_PANEL
    while n_cols % panel:
        panel -= HDIM
    rows = max(SUBLANES, WEIGHT_STAGE_BYTES // (4 * panel))
    while n_rows % rows:
        rows //= 2
    return rows, panel


def _weight_scratch(n_rows, n_cols):
    return [pltpu.VMEM((n_rows, n_cols), BF16),
            pltpu.VMEM((2,) + _stage_shape(n_rows, n_cols), F32),
            pltpu.SemaphoreType.DMA((2,))]


def _load_weight(w_hbm, col0, w_ref, stage_ref, sem_ref):
    n_rows, n_cols = w_ref.shape
    _, rows, panel = stage_ref.shape
    chunks = [(r0, c0) for c0 in range(0, n_cols, panel) for r0 in range(0, n_rows, rows)]

    def copy(i):
        r0, c0 = chunks[i]
        return pltpu.make_async_copy(
            w_hbm.at[pl.ds(r0, rows), pl.ds(col0 + c0, panel)],
            stage_ref.at[i % 2], sem_ref.at[i % 2])

    copy(0).start()
    for i, (r0, c0) in enumerate(chunks):
        if i + 1 < len(chunks):
            copy(i + 1).start()
        copy(i).wait()
        w_ref[r0:r0 + rows, c0:c0 + panel] = stage_ref[i % 2].astype(BF16)


def _inproj_kernel(x_ref, gmix_ref, w_hbm, lbp_ref, qg_ref, kg_ref, tri_ref, out_ref, b_ref,
                   w_ref, stage_ref, sem_ref):
    @pl.when(pl.program_id(0) == 0)
    def _():
        _load_weight(w_hbm, 0, w_ref, stage_ref, sem_ref)

    h = (_rms(x_ref[...]) * gmix_ref[...]).astype(BF16)

    def proj(j):
        return _dot(h, w_ref[:, j * GROUP_W:(j + 1) * GROUP_W])

    def put(j, val):
        out_ref[:, j * GROUP_W:(j + 1) * GROUP_W] = val.astype(BF16)

    a = proj(0)
    put(0, a * _sigmoid(a))

    a = proj(1)
    p0 = lbp_ref[0:1, :]
    p1 = lbp_ref[1:2, :]
    pm = jnp.maximum(p0, p1)
    e0 = jnp.exp(p0 - pm)
    e1 = jnp.exp(p1 - pm)
    lb = e0 / (e0 + e1)
    s = _sigmoid(a)
    put(1, (1.0 - lb) * (1.0 - s))
    g = jnp.log2(lb + (1.0 - lb) * s)
    g_hi = g.astype(BF16)
    g_lo = (g - g_hi.astype(F32)).astype(BF16)
    b_ref[...] = _dot(tri_ref[...], g_hi) + _dot(tri_ref[...], g_lo)

    put(2, proj(2))

    a = proj(3)
    put(3, a * _sigmoid(a))

    scale = HDIM ** -0.5 * LOG2E
    for j, gain_ref, mul in ((4, qg_ref, scale), (5, kg_ref, 1.0)):
        a = proj(j)
        gain = gain_ref[...] * mul
        for hd in range(HEADS):
            ah = a[:, hd * HDIM:(hd + 1) * HDIM]
            c0 = j * GROUP_W + hd * HDIM
            out_ref[:, c0:c0 + HDIM] = (_rms(ah) * gain).astype(BF16)

    put(6, proj(6))


def _inproj(x2, g_mix, w_in, hg_lb, q_gain, k_gain, tm):
    S, D = x2.shape
    n_out = w_in.shape[1]
    full = lambda shape: pl.BlockSpec(shape, lambda i: (0, 0))
    return pl.pallas_call(
        _inproj_kernel,
        grid=(S // tm,),
        in_specs=[
            pl.BlockSpec((tm, D), lambda i: (i, 0)),
            full((1, D)),
            pl.BlockSpec(memory_space=pl.ANY),
            full(hg_lb.shape),
            full((1, HDIM)),
            full((1, HDIM)),
            full((tm, tm)),
        ],
        out_specs=[
            pl.BlockSpec((tm, n_out), lambda i: (i, 0)),
            pl.BlockSpec((tm, GROUP_W), lambda i: (i, 0)),
        ],
        out_shape=[
            jax.ShapeDtypeStruct((S, n_out), BF16),
            jax.ShapeDtypeStruct((S, GROUP_W), F32),
        ],
        scratch_shapes=_weight_scratch(D, n_out),
        compiler_params=pltpu.CompilerParams(
            dimension_semantics=("arbitrary",), vmem_limit_bytes=VMEM_LIMIT),
        name="inproj",
    )(x2, g_mix, w_in, hg_lb, q_gain, k_gain, _tril(tm))


HG_BLOCK = 256
HG_HEADS_PER_STEP = 8
HG_HIGH = (128, 64, 32, 16, 8)
HG_LOW = (4, 2, 1)


def _hgrn_kernel(mlow_ref, q_ref, k_ref, v_ref, b_ref, gate_ref, gain_ref, o_ref, state_ref):
    T = HG_BLOCK
    H = T // 2
    NH = HG_HEADS_PER_STEP

    @pl.when(pl.program_id(1) == 0)
    def _():
        state_ref[...] = jnp.zeros_like(state_ref)

    row = lax.broadcasted_iota(jnp.int32, (T, HDIM), 0)
    tt = lax.broadcasted_iota(jnp.int32, (T, T), 0)
    ss = lax.broadcasted_iota(jnp.int32, (T, T), 1)
    hs = lax.broadcasted_iota(jnp.int32, (H, T), 1)
    diag = tt == ss
    same2, same4, same8 = ((tt // n) == (ss // n) for n in (2, 4, 8))
    heads = range(NH)
    col = lambda hd: slice(hd * HDIM, (hd + 1) * HDIM)
    second_half = [mlow_ref[li] for li in range(len(HG_LOW))]
    first_half = [1.0 - mk for mk in second_half]
    q = [q_ref[:, col(hd)] for hd in heads]
    k = [k_ref[:, col(hd)] for hd in heads]
    v = [v_ref[:, col(hd)] for hd in heads]
    b = [b_ref[:, col(hd)] for hd in heads]

    inter = []
    for hd in heads:
        state = state_ref[hd]
        b_last = b[hd][T - 1:T, :]
        inter.append(_dot_nt(q[hd] * jnp.exp2(b[hd]).astype(BF16), state.astype(BF16)))
        k_out = k[hd] * jnp.exp2(b_last - b[hd]).astype(BF16)
        state_ref[hd] = state * jnp.exp2(b_last) + _dot_tn(v[hd], k_out)

    attn = []
    for hd in heads:
        bh = b[hd]
        prev1 = bh - pltpu.roll(bh, 1, 0)
        prev2 = bh - pltpu.roll(bh, 2, 0)
        next1 = pltpu.roll(bh, T - 1, 0) - bh
        b3 = bh.reshape(T // SUBLANES, SUBLANES, HDIM)
        mid = jnp.broadcast_to(b3[:, 3:4, :], b3.shape).reshape(T, HDIM)
        pos = row & 3
        exps = {
            4: jnp.where((row & 4) != 0, bh - mid, mid - bh),
            2: jnp.where(pos == 2, prev1, jnp.where(pos == 3, prev2,
                                                   jnp.where(pos == 0, next1, 0.0))),
            1: jnp.where((row & 1) != 0, prev1, 0.0),
        }
        prods = {}
        for li, m in enumerate(HG_LOW):
            dec = jnp.exp2(exps[m]).astype(BF16)
            q_m = q[hd] * (dec * second_half[li])
            k_m = k[hd] * (dec * first_half[li])
            prods[m] = _dot_nt(q_m, k_m)
        pd = _dot_nt(q[hd], k[hd])
        attn.append(jnp.where(same2, jnp.where(diag, pd, prods[1]),
                              jnp.where(same4, prods[2], jnp.where(same8, prods[4], 0.0))))

    for m in HG_HIGH:
        nb = T // (2 * m)
        split = lambda a: a.reshape(nb, 2 * m, HDIM)
        second = lambda a: split(a)[:, m:, :].reshape(H, HDIM)
        keep = ((lax.broadcasted_iota(jnp.int32, (H, T), 0) // m) == (hs // (2 * m))) & (
            (hs & m) == 0)
        for hd in heads:
            b3 = split(b[hd])
            mid = b3[:, m - 1:m, :]
            dec_b = jnp.exp2(b3[:, m:, :] - mid).reshape(H, HDIM)
            dec_a = jnp.exp2(mid - b3[:, :m, :])
            if m >= 2 * SUBLANES:
                q_b = second(q[hd]) * dec_b.astype(BF16)
                k_a = split(k[hd])[:, :m, :] * dec_a.astype(BF16)
            else:
                q_b = (second(q[hd].astype(F32)) * dec_b).astype(BF16)
                k_a = (split(k[hd].astype(F32))[:, :m, :] * dec_a).astype(BF16)
            k_a = jnp.concatenate([k_a, jnp.zeros_like(k_a)], axis=1).reshape(T, HDIM)
            p = _dot_nt(q_b, k_a)
            a3 = attn[hd].reshape(nb, 2 * m, T)
            upd = jnp.where(keep, p, a3[:, m:, :].reshape(H, T)).reshape(nb, m, T)
            attn[hd] = jnp.concatenate([a3[:, :m, :], upd], axis=1).reshape(T, T)

    for hd in heads:
        o = inter[hd] + _dot(attn[hd].astype(BF16), v[hd])
        o = (_rms(o) * gain_ref[:, col(hd)]).astype(BF16) * gate_ref[:, col(hd)]
        o_ref[:, col(hd)] = o.astype(o_ref.dtype)


def _hgrn(proj, b, out_gain):
    S = proj.shape[0]
    T, hp = HG_BLOCK, HG_HEADS_PER_STEP
    w = hp * HDIM
    col = lambda off: pl.BlockSpec((T, w), lambda h, i: (i, off // hp + h))
    t = np.arange(T)
    mlow = jnp.asarray(np.stack([np.broadcast_to(((t & m) != 0)[:, None], (T, HDIM))
                                 for m in HG_LOW]).astype(np.float32), BF16)
    return pl.pallas_call(
        _hgrn_kernel,
        grid=(HEADS // hp, S // T),
        in_specs=[
            pl.BlockSpec(mlow.shape, lambda h, i: (0, 0, 0)),
            col(COL_Q_HG), col(COL_K_HG), col(COL_V_HG), col(0), col(COL_GATE),
            pl.BlockSpec((1, w), lambda h, i: (0, h)),
        ],
        out_specs=pl.BlockSpec((T, w), lambda h, i: (i, h)),
        out_shape=jax.ShapeDtypeStruct((S, GROUP_W), BF16),
        scratch_shapes=[pltpu.VMEM((hp, HDIM, HDIM), F32)],
        compiler_params=pltpu.CompilerParams(
            dimension_semantics=("arbitrary", "arbitrary"), vmem_limit_bytes=VMEM_LIMIT),
        name="hgrn2",
    )(mlow, proj, proj, proj, b, proj, out_gain)


def _bias_kernel(row_ref, o_ref):
    x = jnp.broadcast_to(row_ref[0], (ATT_SUB, BIAS_EXT))
    y = pltpu.roll(x, 0, 1, stride=1, stride_axis=0)[:, :ATT_KEYS]
    r = lax.broadcasted_iota(jnp.int32, (ATT_SUB, ATT_KEYS), 0)
    u = lax.broadcasted_iota(jnp.int32, (ATT_SUB, ATT_KEYS), 1)
    lo = jnp.where(r < CHUNK, 0, CHUNK)
    visible = (u >= lo) & (u < lo + BAND_CHUNKS * CHUNK)
    o_ref[0] = jnp.where(visible, y * LOG2E, -jnp.inf)


def _rel_bias(rel_bias):
    far = rel_bias[:, 2 * REL_CLIP:]
    pad = BAND_CHUNKS * CHUNK - REL_CLIP - CHUNK
    ext = jnp.concatenate([
        jnp.broadcast_to(far, (HEADS, pad)),
        rel_bias[:, :0:-1],
        jnp.broadcast_to(far, (HEADS, BIAS_EXT - pad - 2 * REL_CLIP)),
    ], axis=1).reshape(HEADS, 1, BIAS_EXT)
    return pl.pallas_call(
        _bias_kernel,
        grid=(HEADS,),
        in_specs=[pl.BlockSpec((1, 1, BIAS_EXT), lambda h: (h, 0, 0))],
        out_specs=pl.BlockSpec((1, ATT_SUB, ATT_KEYS), lambda h: (h, 0, 0)),
        out_shape=jax.ShapeDtypeStruct((HEADS, ATT_SUB, ATT_KEYS), F32),
        name="rel_bias",
    )(ext)


ATT_HEADS_PER_STEP = 4
ATT_WAVE = 2


def _attn_body(bias_ref, q_ref, kp_ref, kc_ref, vp_ref, vc_ref, o_ref, first):
    n_sub = ATT_BLOCK // ATT_SUB
    u = lax.broadcasted_iota(jnp.int32, (ATT_SUB, ATT_KEYS), 1)
    jobs = [(hd, j) for hd in range(ATT_HEADS_PER_STEP) for j in range(n_sub)]
    col = lambda hd: slice(hd * HDIM, (hd + 1) * HDIM)
    rows = lambda j: slice(j * ATT_SUB, (j + 1) * ATT_SUB)
    keys = lambda j: slice(j * ATT_SUB, j * ATT_SUB + ATT_KEYS)
    kcat = [jnp.concatenate([kp_ref[:, col(hd)], kc_ref[:, col(hd)]], axis=0)
            for hd in range(ATT_HEADS_PER_STEP)]
    vcat = [jnp.concatenate([vp_ref[:, col(hd)], vc_ref[:, col(hd)]], axis=0)
            for hd in range(ATT_HEADS_PER_STEP)]

    def qk(hd, j):
        s = _dot_nt(q_ref[rows(j), col(hd)], kcat[hd][keys(j)]) + bias_ref[hd]
        if first:
            s = jnp.where(u < ATT_BLOCK - j * ATT_SUB, -jnp.inf, s)
        return s

    def softmax(s):
        p = jnp.exp2(s - jnp.max(s, axis=-1, keepdims=True))
        return p.astype(BF16), jnp.sum(p, axis=-1, keepdims=True)

    def pv(hd, j, p, l):
        o = _dot(p, vcat[hd][keys(j)]) / l
        o_ref[rows(j), col(hd)] = o.astype(o_ref.dtype)

    waves = [jobs[i:i + ATT_WAVE] for i in range(0, len(jobs), ATT_WAVE)]
    scores = [qk(hd, j) for hd, j in waves[0]]
    for w, wave in enumerate(waves):
        nxt = [qk(hd, j) for hd, j in waves[w + 1]] if w + 1 < len(waves) else None
        for (hd, j), s in zip(wave, scores):
            pv(hd, j, *softmax(s))
        scores = nxt


def _attn_kernel(*refs):
    first = pl.program_id(1) == 0

    @pl.when(first)
    def _():
        _attn_body(*refs, first=True)

    @pl.when(jnp.logical_not(first))
    def _():
        _attn_body(*refs, first=False)


def _attention(proj, bias):
    S = proj.shape[0]
    hp = ATT_HEADS_PER_STEP
    w = hp * HDIM
    cur = lambda off: pl.BlockSpec((ATT_BLOCK, w), lambda h, i: (i, off // hp + h))
    prev = lambda off: pl.BlockSpec(
        (ATT_BLOCK, w), lambda h, i: (jnp.maximum(i - 1, 0), off // hp + h))
    return pl.pallas_call(
        _attn_kernel,
        grid=(HEADS // hp, S // ATT_BLOCK),
        in_specs=[
            pl.BlockSpec((hp, ATT_SUB, ATT_KEYS), lambda h, i: (h, 0, 0)),
            cur(COL_Q_AT), prev(COL_K_AT), cur(COL_K_AT), prev(COL_V_AT), cur(COL_V_AT),
        ],
        out_specs=pl.BlockSpec((ATT_BLOCK, w), lambda h, i: (i, h)),
        out_shape=jax.ShapeDtypeStruct((S, GROUP_W), BF16),
        compiler_params=pltpu.CompilerParams(
            dimension_semantics=("arbitrary", "arbitrary"), vmem_limit_bytes=VMEM_LIMIT),
        name="band_attn",
    )(bias, proj, proj, proj, proj, proj)


def _outproj_kernel(ohg_ref, oat_ref, x_ref, w_hbm, again_ref, gffn_ref, x1_ref, h2_ref,
                    w_ref, stage_ref, sem_ref, perm0_ref, perm1_ref, *, sub):
    assert sub == PERM_TILE
    perm_refs = (perm0_ref, perm1_ref)

    @pl.when(pl.program_id(0) == 0)
    def _():
        _load_weight(w_hbm, 0, w_ref, stage_ref, sem_ref)

    for r0 in range(0, x_ref.shape[0], sub):
        rows = slice(r0, r0 + sub)
        oat = (_rms(oat_ref[rows, :].astype(F32)) * again_ref[...]).astype(BF16)
        x1 = (x_ref[rows, :] + _dot(ohg_ref[rows, :], w_ref[0:GROUP_W, :])
              + _dot(oat, w_ref[GROUP_W:, :]))
        x1_ref[rows, :] = x1
        h2 = _rms(x1) * gffn_ref[...]
        perm_ref = perm_refs[(r0 // sub) % 2]
        for c in range(h2.shape[1] // HDIM):
            for s in range(SUBLANES):
                perm_ref[c, pl.ds(s, PERM_R, stride=SUBLANES), :] = (
                    h2[s * PERM_R:(s + 1) * PERM_R, c * HDIM:(c + 1) * HDIM])
        for c in range(h2.shape[1] // HDIM):
            h2_ref[rows, c * HDIM:(c + 1) * HDIM] = perm_ref[c].astype(BF16)


def _outproj(o_hg, o_at, x2, w_out, at_gain, g_ffn, tm):
    S, D = x2.shape
    full = lambda shape: pl.BlockSpec(shape, lambda i: (0, 0))
    rows = lambda w: pl.BlockSpec((tm, w), lambda i: (i, 0))
    return pl.pallas_call(
        functools.partial(_outproj_kernel, sub=tm // 2),
        grid=(S // tm,),
        in_specs=[rows(GROUP_W), rows(GROUP_W), rows(D),
                  pl.BlockSpec(memory_space=pl.ANY),
                  full((1, GROUP_W)), full((1, D))],
        out_specs=[rows(D), rows(D)],
        out_shape=[jax.ShapeDtypeStruct((S, D), F32), jax.ShapeDtypeStruct((S, D), BF16)],
        scratch_shapes=(_weight_scratch(*w_out.shape)
                        + [pltpu.VMEM((D // HDIM, PERM_TILE, HDIM), F32)] * 2),
        compiler_params=pltpu.CompilerParams(
            dimension_semantics=("arbitrary",), vmem_limit_bytes=VMEM_LIMIT),
        name="outproj",
    )(o_hg, o_at, x2, w_out, at_gain, g_ffn)


def _shift1(x, before):
    sub = lax.broadcasted_iota(jnp.int32, (SUBLANES, x.shape[1]), 0)
    top = jnp.where(sub == 0, before, pltpu.roll(x[-SUBLANES:], 1, 0))
    return jnp.concatenate([top, x[:-SUBLANES]], axis=0)


def _ffn_up_kernel(h_ref, w_hbm, cwa_ref, cwg_ref, cba_ref, cbg_ref, o_ref,
                   ta_ref, tg_ref, wa_ref, stage_a, sem_a, wg_ref, stage_g, sem_g,
                   perm0_ref, perm1_ref, *, cblk):
    tm, tn = o_ref.shape
    assert tm == PERM_TILE
    perm_refs = (perm0_ref, perm1_ref)

    @pl.when(pl.program_id(1) == 0)
    def _():
        ta_ref[...] = jnp.zeros_like(ta_ref)
        tg_ref[...] = jnp.zeros_like(tg_ref)
        slab = pl.program_id(0)
        d_ff = w_hbm.shape[1] // 2
        _load_weight(w_hbm, pl.multiple_of(slab * tn, HDIM), wa_ref, stage_a, sem_a)
        _load_weight(w_hbm, pl.multiple_of(d_ff + slab * tn, HDIM), wg_ref, stage_g, sem_g)

    def conv(u, t_ref, cw_ref, cb_ref, cols, scale):
        cw = cw_ref[:, cols] * scale
        cb = cb_ref[:, cols] * scale
        last = t_ref[2 * SUBLANES - 1:2 * SUBLANES, cols]
        last2 = t_ref[SUBLANES - 1:SUBLANES, cols]
        z = _shift1(cw[0:1, :] * u, cw[0:1, :] * last) + cw[1:2, :] * u
        y = _shift1(z, cw[0:1, :] * last2 + cw[1:2, :] * last) + (cw[2:3, :] * u + cb)
        t_ref[:, cols] = u[tm - 2 * SUBLANES:tm]
        return y

    h = h_ref[...]
    for bi, c0 in enumerate(range(0, tn, cblk)):
        cols = slice(c0, c0 + cblk)
        ya = conv(_dot(h, wa_ref[:, cols]), ta_ref, cwa_ref, cba_ref, cols, 1.0)
        yh = conv(_dot(h, wg_ref[:, cols]), tg_ref, cwg_ref, cbg_ref, cols, 0.5)
        act = ya * (yh * (1.0 + jnp.tanh(yh)))
        perm_ref = perm_refs[bi % 2]
        for l in range(cblk // HDIM):
            perm_ref[l] = act[:, l * HDIM:(l + 1) * HDIM]
            lanes = slice(c0 + l * HDIM, c0 + (l + 1) * HDIM)
            for s in range(SUBLANES):
                o_ref[s * PERM_R:(s + 1) * PERM_R, lanes] = (
                    perm_ref[l, pl.ds(s, PERM_R, stride=SUBLANES), :].astype(o_ref.dtype))


def _ffn_up(h2, w_up, conv_w, conv_b, tm, n_slabs, cblk):
    S, D = h2.shape
    d_ff = w_up.shape[1] // 2
    tn = d_ff // n_slabs
    a_col = lambda r: pl.BlockSpec((r, tn), lambda c, m: (0, c))
    g_col = lambda r: pl.BlockSpec((r, tn), lambda c, m: (0, n_slabs + c))
    return pl.pallas_call(
        functools.partial(_ffn_up_kernel, cblk=cblk),
        grid=(n_slabs, S // tm),
        in_specs=[pl.BlockSpec((tm, D), lambda c, m: (m, 0)),
                  pl.BlockSpec(memory_space=pl.ANY),
                  a_col(CONV_W), g_col(CONV_W), a_col(1), g_col(1)],
        out_specs=pl.BlockSpec((tm, tn), lambda c, m: (m, c)),
        out_shape=jax.ShapeDtypeStruct((S, d_ff), BF16),
        scratch_shapes=([pltpu.VMEM((2 * SUBLANES, tn), F32)] * 2 + 2 * _weight_scratch(D, tn)
                        + [pltpu.VMEM((cblk // HDIM, tm, HDIM), F32)] * 2),
        compiler_params=pltpu.CompilerParams(
            dimension_semantics=("arbitrary", "arbitrary"), vmem_limit_bytes=VMEM_LIMIT),
        name="ffn_up",
    )(h2, w_up, conv_w, conv_w, conv_b, conv_b)


def _ffn_down_kernel(a_ref, w_hbm, x1_ref, o_ref, w_ref, stage_ref, sem_ref):
    @pl.when(pl.program_id(0) == 0)
    def _():
        _load_weight(w_hbm, 0, w_ref, stage_ref, sem_ref)

    o_ref[...] = x1_ref[...] + _dot(a_ref[...], w_ref[...])


def _ffn_down(act, w_down, x1, tm):
    S, D = x1.shape
    d_ff = act.shape[1]
    return pl.pallas_call(
        _ffn_down_kernel,
        grid=(S // tm,),
        in_specs=[pl.BlockSpec((tm, d_ff), lambda i: (i, 0)),
                  pl.BlockSpec(memory_space=pl.ANY),
                  pl.BlockSpec((tm, D), lambda i: (i, 0))],
        out_specs=pl.BlockSpec((tm, D), lambda i: (i, 0)),
        out_shape=jax.ShapeDtypeStruct((S, D), F32),
        scratch_shapes=_weight_scratch(d_ff, D),
        compiler_params=pltpu.CompilerParams(
            dimension_semantics=("arbitrary",), vmem_limit_bytes=VMEM_LIMIT),
        name="ffn_down",
    )(act, w_down, x1)


def kernel(x, g_mix, w_in, hg_lb, hg_out_gain, q_gain, k_gain, rel_bias, at_out_gain, w_out,
           g_ffn, w_up, conv_w, conv_b, w_down):
    B, S, D = x.shape
    depth = g_mix.shape[0]
    assert B == 1 and depth == 1 and S % 1024 == 0
    assert w_in.shape[2] == N_GROUPS * GROUP_W
    x2 = x.reshape(S, D)
    proj, b = _inproj(x2, g_mix, w_in[0], hg_lb, q_gain, k_gain, tm=HG_BLOCK)
    o_hg = _hgrn(proj, b, hg_out_gain)
    o_at = _attention(proj, _rel_bias(rel_bias[0]))
    x1, h2 = _outproj(o_hg, o_at, x2, w_out[0], at_out_gain, g_ffn, tm=512)
    act = _ffn_up(h2, w_up[0], conv_w[0], conv_b, tm=256, n_slabs=2, cblk=256)
    out = _ffn_down(act, w_down[0], x1, tm=512)
    return out.reshape(B, S, D)
```

```python
import functools

import jax
import jax.numpy as jnp
import numpy as np
from jax import lax
from jax.experimental import pallas as pl
from jax.experimental.pallas import tpu as pltpu

F32 = jnp.float32
BF16 = jnp.bfloat16

CHUNK = 64
EPS = 1e-6
HEADS = 8
HDIM = 128
GROUP_W = HEADS * HDIM
N_GROUPS = 7
BAND_CHUNKS = 9
REL_CLIP = 128
CONV_W = 3
SUBLANES = 8
LOG2E = 1.4426950408889634
PERM_TILE = 256
PERM_R = PERM_TILE // SUBLANES

COL_Q_HG, COL_K_HG, COL_V_HG, COL_GATE, COL_Q_AT, COL_K_AT, COL_V_AT = (
    0, 8, 16, 24, 32, 40, 48)

ATT_BLOCK = 512
ATT_SUB = 128
ATT_KEYS = ATT_SUB + (BAND_CHUNKS - 1) * CHUNK
BIAS_EXT = 768

VMEM_LIMIT = 56 * 1024 * 1024
VMEM_LIMIT_BIG = 61 * 1024 * 1024


def _dot(a, b):
    return jnp.dot(a, b, preferred_element_type=F32)


def _dot_nt(a, b):
    return lax.dot_general(a, b, (((1,), (1,)), ((), ())), preferred_element_type=F32)


def _dot_tn(a, b):
    return lax.dot_general(a, b, (((0,), (0,)), ((), ())), preferred_element_type=F32)


def _sigmoid(x):
    return 1.0 / (1.0 + jnp.exp(-x))


def _tril(n):
    return jnp.asarray(np.tril(np.ones((n, n), np.float32)), BF16)


def _rms(x):
    return x * lax.rsqrt(jnp.mean(x * x, axis=-1, keepdims=True) + EPS)


WEIGHT_STAGE_BYTES = 2 * 1024 * 1024
WEIGHT_PANEL = 1024


def _stage_shape(n_rows, n_cols):
    panel = WEIGHT_PANEL
    while n_cols % panel:
        panel -= HDIM
    rows = max(SUBLANES, WEIGHT_STAGE_BYTES // (4 * panel))
    while n_rows % rows:
        rows //= 2
    return rows, panel


def _weight_scratch(n_rows, n_cols):
    return [pltpu.VMEM((n_rows, n_cols), BF16),
            pltpu.VMEM((2,) + _stage_shape(n_rows, n_cols), F32),
            pltpu.SemaphoreType.DMA((2,))]


def _load_weight(w_hbm, col0, w_ref, stage_ref, sem_ref):
    n_rows, n_cols = w_ref.shape
    _, rows, panel = stage_ref.shape
    chunks = [(r0, c0) for c0 in range(0, n_cols, panel) for r0 in range(0, n_rows, rows)]

    def copy(i):
        r0, c0 = chunks[i]
        return pltpu.make_async_copy(
            w_hbm.at[pl.ds(r0, rows), pl.ds(col0 + c0, panel)],
            stage_ref.at[i % 2], sem_ref.at[i % 2])

    copy(0).start()
    for i, (r0, c0) in enumerate(chunks):
        if i + 1 < len(chunks):
            copy(i + 1).start()
        copy(i).wait()
        w_ref[r0:r0 + rows, c0:c0 + panel] = stage_ref[i % 2].astype(BF16)


def _inproj_kernel(x_ref, gmix_ref, w_hbm, lbp_ref, qg_ref, kg_ref, tri_ref, out_ref, b_ref,
                   w_ref, stage_ref, sem_ref):
    @pl.when(pl.program_id(0) == 0)
    def _():
        _load_weight(w_hbm, 0, w_ref, stage_ref, sem_ref)

    h = (_rms(x_ref[...]) * gmix_ref[...]).astype(BF16)

    def proj(j):
        return _dot(h, w_ref[:, j * GROUP_W:(j + 1) * GROUP_W])

    def put(j, val):
        out_ref[:, j * GROUP_W:(j + 1) * GROUP_W] = val.astype(BF16)

    a = proj(0)
    put(0, a * _sigmoid(a))

    a = proj(1)
    p0 = lbp_ref[0:1, :]
    p1 = lbp_ref[1:2, :]
    pm = jnp.maximum(p0, p1)
    e0 = jnp.exp(p0 - pm)
    e1 = jnp.exp(p1 - pm)
    lb = e0 / (e0 + e1)
    s = _sigmoid(a)
    put(1, (1.0 - lb) * (1.0 - s))
    g = jnp.log2(lb + (1.0 - lb) * s)
    g_hi = g.astype(BF16)
    g_lo = (g - g_hi.astype(F32)).astype(BF16)
    b_ref[...] = _dot(tri_ref[...], g_hi) + _dot(tri_ref[...], g_lo)

    put(2, proj(2))

    a = proj(3)
    put(3, a * _sigmoid(a))

    scale = HDIM ** -0.5 * LOG2E
    for j, gain_ref, mul in ((4, qg_ref, scale), (5, kg_ref, 1.0)):
        a = proj(j)
        gain = gain_ref[...] * mul
        for hd in range(HEADS):
            ah = a[:, hd * HDIM:(hd + 1) * HDIM]
            c0 = j * GROUP_W + hd * HDIM
            out_ref[:, c0:c0 + HDIM] = (_rms(ah) * gain).astype(BF16)

    put(6, proj(6))


def _inproj(x2, g_mix, w_in, hg_lb, q_gain, k_gain, tm):
    S, D = x2.shape
    n_out = w_in.shape[1]
    full = lambda shape: pl.BlockSpec(shape, lambda i: (0, 0))
    return pl.pallas_call(
        _inproj_kernel,
        grid=(S // tm,),
        in_specs=[
            pl.BlockSpec((tm, D), lambda i: (i, 0)),
            full((1, D)),
            pl.BlockSpec(memory_space=pl.ANY),
            full(hg_lb.shape),
            full((1, HDIM)),
            full((1, HDIM)),
            full((tm, tm)),
        ],
        out_specs=[
            pl.BlockSpec((tm, n_out), lambda i: (i, 0)),
            pl.BlockSpec((tm, GROUP_W), lambda i: (i, 0)),
        ],
        out_shape=[
            jax.ShapeDtypeStruct((S, n_out), BF16),
            jax.ShapeDtypeStruct((S, GROUP_W), F32),
        ],
        scratch_shapes=_weight_scratch(D, n_out),
        compiler_params=pltpu.CompilerParams(
            dimension_semantics=("arbitrary",), vmem_limit_bytes=VMEM_LIMIT),
        name="inproj",
    )(x2, g_mix, w_in, hg_lb, q_gain, k_gain, _tril(tm))


HG_BLOCK = 256
HG_HEADS_PER_STEP = 8
HG_HIGH = (128, 64, 32, 16, 8)
HG_LOW = (4, 2, 1)


def _hgrn_kernel(mlow_ref, q_ref, k_ref, v_ref, b_ref, gate_ref, gain_ref, o_ref, state_ref):
    T = HG_BLOCK
    H = T // 2
    NH = HG_HEADS_PER_STEP

    @pl.when(pl.program_id(1) == 0)
    def _():
        state_ref[...] = jnp.zeros_like(state_ref)

    row = lax.broadcasted_iota(jnp.int32, (T, HDIM), 0)
    tt = lax.broadcasted_iota(jnp.int32, (T, T), 0)
    ss = lax.broadcasted_iota(jnp.int32, (T, T), 1)
    hs = lax.broadcasted_iota(jnp.int32, (H, T), 1)
    diag = tt == ss
    same2, same4, same8 = ((tt // n) == (ss // n) for n in (2, 4, 8))
    heads = range(NH)
    col = lambda hd: slice(hd * HDIM, (hd + 1) * HDIM)
    second_half = [mlow_ref[li] for li in range(len(HG_LOW))]
    first_half = [1.0 - mk for mk in second_half]
    q = [q_ref[:, col(hd)] for hd in heads]
    k = [k_ref[:, col(hd)] for hd in heads]
    v = [v_ref[:, col(hd)] for hd in heads]
    b = [b_ref[:, col(hd)] for hd in heads]

    inter = []
    for hd in heads:
        state = state_ref[hd]
        b_last = b[hd][T - 1:T, :]
        inter.append(_dot_nt(q[hd] * jnp.exp2(b[hd]).astype(BF16), state.astype(BF16)))
        k_out = k[hd] * jnp.exp2(b_last - b[hd]).astype(BF16)
        state_ref[hd] = state * jnp.exp2(b_last) + _dot_tn(v[hd], k_out)

    attn = []
    for hd in heads:
        bh = b[hd]
        prev1 = bh - pltpu.roll(bh, 1, 0)
        prev2 = bh - pltpu.roll(bh, 2, 0)
        next1 = pltpu.roll(bh, T - 1, 0) - bh
        b3 = bh.reshape(T // SUBLANES, SUBLANES, HDIM)
        mid = jnp.broadcast_to(b3[:, 3:4, :], b3.shape).reshape(T, HDIM)
        pos = row & 3
        exps = {
            4: jnp.where((row & 4) != 0, bh - mid, mid - bh),
            2: jnp.where(pos == 2, prev1, jnp.where(pos == 3, prev2,
                                                   jnp.where(pos == 0, next1, 0.0))),
            1: jnp.where((row & 1) != 0, prev1, 0.0),
        }
        prods = {}
        for li, m in enumerate(HG_LOW):
            dec = jnp.exp2(exps[m]).astype(BF16)
            q_m = q[hd] * (dec * second_half[li])
            k_m = k[hd] * (dec * first_half[li])
            prods[m] = _dot_nt(q_m, k_m)
        pd = _dot_nt(q[hd], k[hd])
        attn.append(jnp.where(same2, jnp.where(diag, pd, prods[1]),
                              jnp.where(same4, prods[2], jnp.where(same8, prods[4], 0.0))))

    for m in HG_HIGH:
        nb = T // (2 * m)
        split = lambda a: a.reshape(nb, 2 * m, HDIM)
        second = lambda a: split(a)[:, m:, :].reshape(H, HDIM)
        keep = ((lax.broadcasted_iota(jnp.int32, (H, T), 0) // m) == (hs // (2 * m))) & (
            (hs & m) == 0)
        for hd in heads:
            b3 = split(b[hd])
            mid = b3[:, m - 1:m, :]
            dec_b = jnp.exp2(b3[:, m:, :] - mid).reshape(H, HDIM)
            dec_a = jnp.exp2(mid - b3[:, :m, :])
            if m >= 2 * SUBLANES:
                q_b = second(q[hd]) * dec_b.astype(BF16)
                k_a = split(k[hd])[:, :m, :] * dec_a.astype(BF16)
            else:
                q_b = (second(q[hd].astype(F32)) * dec_b).astype(BF16)
                k_a = (split(k[hd].astype(F32))[:, :m, :] * dec_a).astype(BF16)
            k_a = jnp.concatenate([k_a, jnp.zeros_like(k_a)], axis=1).reshape(T, HDIM)
            p = _dot_nt(q_b, k_a)
            a3 = attn[hd].reshape(nb, 2 * m, T)
            upd = jnp.where(keep, p, a3[:, m:, :].reshape(H, T)).reshape(nb, m, T)
            attn[hd] = jnp.concatenate([a3[:, :m, :], upd], axis=1).reshape(T, T)

    for hd in heads:
        o = inter[hd] + _dot(attn[hd].astype(BF16), v[hd])
        o = (_rms(o) * gain_ref[:, col(hd)]).astype(BF16) * gate_ref[:, col(hd)]
        o_ref[:, col(hd)] = o.astype(o_ref.dtype)


def _hgrn(proj, b, out_gain):
    S = proj.shape[0]
    T, hp = HG_BLOCK, HG_HEADS_PER_STEP
    w = hp * HDIM
    col = lambda off: pl.BlockSpec((T, w), lambda h, i: (i, off // hp + h))
    t = np.arange(T)
    mlow = jnp.asarray(np.stack([np.broadcast_to(((t & m) != 0)[:, None], (T, HDIM))
                                 for m in HG_LOW]).astype(np.float32), BF16)
    return pl.pallas_call(
        _hgrn_kernel,
        grid=(HEADS // hp, S // T),
        in_specs=[
            pl.BlockSpec(mlow.shape, lambda h, i: (0, 0, 0)),
            col(COL_Q_HG), col(COL_K_HG), col(COL_V_HG), col(0), col(COL_GATE),
            pl.BlockSpec((1, w), lambda h, i: (0, h)),
        ],
        out_specs=pl.BlockSpec((T, w), lambda h, i: (i, h)),
        out_shape=jax.ShapeDtypeStruct((S, GROUP_W), BF16),
        scratch_shapes=[pltpu.VMEM((hp, HDIM, HDIM), F32)],
        compiler_params=pltpu.CompilerParams(
            dimension_semantics=("arbitrary", "arbitrary"), vmem_limit_bytes=VMEM_LIMIT),
        name="hgrn2",
    )(mlow, proj, proj, proj, b, proj, out_gain)


def _bias_kernel(row_ref, o_ref):
    x = jnp.broadcast_to(row_ref[0], (ATT_SUB, BIAS_EXT))
    y = pltpu.roll(x, 0, 1, stride=1, stride_axis=0)[:, :ATT_KEYS]
    r = lax.broadcasted_iota(jnp.int32, (ATT_SUB, ATT_KEYS), 0)
    u = lax.broadcasted_iota(jnp.int32, (ATT_SUB, ATT_KEYS), 1)
    lo = jnp.where(r < CHUNK, 0, CHUNK)
    visible = (u >= lo) & (u < lo + BAND_CHUNKS * CHUNK)
    o_ref[0] = jnp.where(visible, y * LOG2E, -jnp.inf)


def _rel_bias(rel_bias):
    far = rel_bias[:, 2 * REL_CLIP:]
    pad = BAND_CHUNKS * CHUNK - REL_CLIP - CHUNK
    ext = jnp.concatenate([
        jnp.broadcast_to(far, (HEADS, pad)),
        rel_bias[:, :0:-1],
        jnp.broadcast_to(far, (HEADS, BIAS_EXT - pad - 2 * REL_CLIP)),
    ], axis=1).reshape(HEADS, 1, BIAS_EXT)
    return pl.pallas_call(
        _bias_kernel,
        grid=(HEADS,),
        in_specs=[pl.BlockSpec((1, 1, BIAS_EXT), lambda h: (h, 0, 0))],
        out_specs=pl.BlockSpec((1, ATT_SUB, ATT_KEYS), lambda h: (h, 0, 0)),
        out_shape=jax.ShapeDtypeStruct((HEADS, ATT_SUB, ATT_KEYS), F32),
        name="rel_bias",
    )(ext)


ATT_HEADS_PER_STEP = 4
ATT_WAVE = 2


def _attn_body(bias_ref, q_ref, kp_ref, kc_ref, vp_ref, vc_ref, o_ref, first):
    n_sub = ATT_BLOCK // ATT_SUB
    u = lax.broadcasted_iota(jnp.int32, (ATT_SUB, ATT_KEYS), 1)
    jobs = [(hd, j) for hd in range(ATT_HEADS_PER_STEP) for j in range(n_sub)]
    col = lambda hd: slice(hd * HDIM, (hd + 1) * HDIM)
    rows = lambda j: slice(j * ATT_SUB, (j + 1) * ATT_SUB)
    keys = lambda j: slice(j * ATT_SUB, j * ATT_SUB + ATT_KEYS)
    kcat = [jnp.concatenate([kp_ref[:, col(hd)], kc_ref[:, col(hd)]], axis=0)
            for hd in range(ATT_HEADS_PER_STEP)]
    vcat = [jnp.concatenate([vp_ref[:, col(hd)], vc_ref[:, col(hd)]], axis=0)
            for hd in range(ATT_HEADS_PER_STEP)]

    def qk(hd, j):
        s = _dot_nt(q_ref[rows(j), col(hd)], kcat[hd][keys(j)]) + bias_ref[hd]
        if first:
            s = jnp.where(u < ATT_BLOCK - j * ATT_SUB, -jnp.inf, s)
        return s

    def softmax(s):
        p = jnp.exp2(s - jnp.max(s, axis=-1, keepdims=True))
        return p.astype(BF16), jnp.sum(p, axis=-1, keepdims=True)

    def pv(hd, j, p, l):
        o = _dot(p, vcat[hd][keys(j)]) / l
        o_ref[rows(j), col(hd)] = o.astype(o_ref.dtype)

    waves = [jobs[i:i + ATT_WAVE] for i in range(0, len(jobs), ATT_WAVE)]
    scores = [qk(hd, j) for hd, j in waves[0]]
    for w, wave in enumerate(waves):
        nxt = [qk(hd, j) for hd, j in waves[w + 1]] if w + 1 < len(waves) else None
        for (hd, j), s in zip(wave, scores):
            pv(hd, j, *softmax(s))
        scores = nxt


def _attn_kernel(*refs):
    first = pl.program_id(1) == 0

    @pl.when(first)
    def _():
        _attn_body(*refs, first=True)

    @pl.when(jnp.logical_not(first))
    def _():
        _attn_body(*refs, first=False)


def _attention(proj, bias):
    S = proj.shape[0]
    hp = ATT_HEADS_PER_STEP
    w = hp * HDIM
    cur = lambda off: pl.BlockSpec((ATT_BLOCK, w), lambda h, i: (i, off // hp + h))
    prev = lambda off: pl.BlockSpec(
        (ATT_BLOCK, w), lambda h, i: (jnp.maximum(i - 1, 0), off // hp + h))
    return pl.pallas_call(
        _attn_kernel,
        grid=(HEADS // hp, S // ATT_BLOCK),
        in_specs=[
            pl.BlockSpec((hp, ATT_SUB, ATT_KEYS), lambda h, i: (h, 0, 0)),
            cur(COL_Q_AT), prev(COL_K_AT), cur(COL_K_AT), prev(COL_V_AT), cur(COL_V_AT),
        ],
        out_specs=pl.BlockSpec((ATT_BLOCK, w), lambda h, i: (i, h)),
        out_shape=jax.ShapeDtypeStruct((S, GROUP_W), BF16),
        compiler_params=pltpu.CompilerParams(
            dimension_semantics=("arbitrary", "arbitrary"), vmem_limit_bytes=VMEM_LIMIT),
        name="band_attn",
    )(bias, proj, proj, proj, proj, proj)


def _outproj_kernel(ohg_ref, oat_ref, x_ref, w_hbm, again_ref, gffn_ref, x1_ref, h2_ref,
                    w_ref, stage_ref, sem_ref, perm0_ref, perm1_ref, *, sub):
    assert sub == PERM_TILE
    perm_refs = (perm0_ref, perm1_ref)

    @pl.when(pl.program_id(0) == 0)
    def _():
        _load_weight(w_hbm, 0, w_ref, stage_ref, sem_ref)

    for r0 in range(0, x_ref.shape[0], sub):
        rows = slice(r0, r0 + sub)
        oat = (_rms(oat_ref[rows, :].astype(F32)) * again_ref[...]).astype(BF16)
        x1 = (x_ref[rows, :] + _dot(ohg_ref[rows, :], w_ref[0:GROUP_W, :])
              + _dot(oat, w_ref[GROUP_W:, :]))
        x1_ref[rows, :] = x1
        h2 = _rms(x1) * gffn_ref[...]
        perm_ref = perm_refs[(r0 // sub) % 2]
        for c in range(h2.shape[1] // HDIM):
            for s in range(SUBLANES):
                perm_ref[c, pl.ds(s, PERM_R, stride=SUBLANES), :] = (
                    h2[s * PERM_R:(s + 1) * PERM_R, c * HDIM:(c + 1) * HDIM])
        for c in range(h2.shape[1] // HDIM):
            h2_ref[rows, c * HDIM:(c + 1) * HDIM] = perm_ref[c].astype(BF16)


def _outproj(o_hg, o_at, x2, w_out, at_gain, g_ffn, tm):
    S, D = x2.shape
    full = lambda shape: pl.BlockSpec(shape, lambda i: (0, 0))
    rows = lambda w: pl.BlockSpec((tm, w), lambda i: (i, 0))
    return pl.pallas_call(
        functools.partial(_outproj_kernel, sub=tm // 2),
        grid=(S // tm,),
        in_specs=[rows(GROUP_W), rows(GROUP_W), rows(D),
                  pl.BlockSpec(memory_space=pl.ANY),
                  full((1, GROUP_W)), full((1, D))],
        out_specs=[rows(D), rows(D)],
        out_shape=[jax.ShapeDtypeStruct((S, D), F32), jax.ShapeDtypeStruct((S, D), BF16)],
        scratch_shapes=(_weight_scratch(*w_out.shape)
                        + [pltpu.VMEM((D // HDIM, PERM_TILE, HDIM), F32)] * 2),
        compiler_params=pltpu.CompilerParams(
            dimension_semantics=("arbitrary",), vmem_limit_bytes=VMEM_LIMIT),
        name="outproj",
    )(o_hg, o_at, x2, w_out, at_gain, g_ffn)


def _shift1(x, before):
    sub = lax.broadcasted_iota(jnp.int32, (SUBLANES, x.shape[1]), 0)
    top = jnp.where(sub == 0, before, pltpu.roll(x[-SUBLANES:], 1, 0))
    return jnp.concatenate([top, x[:-SUBLANES]], axis=0)


def _ffn_up_kernel(h_ref, w_hbm, cwa_ref, cwg_ref, cba_ref, cbg_ref, o_ref,
                   ta_ref, tg_ref, wa_ref, wg_ref, stage_ref, sem_ref,
                   perm0_ref, perm1_ref, *, cblk):
    tm, tn = o_ref.shape
    assert tm == PERM_TILE
    perm_refs = (perm0_ref, perm1_ref)

    @pl.when(pl.program_id(1) == 0)
    def _():
        ta_ref[...] = jnp.zeros_like(ta_ref)
        tg_ref[...] = jnp.zeros_like(tg_ref)
        slab = pl.program_id(0)
        d_ff = w_hbm.shape[1] // 2
        _load_weight(w_hbm, pl.multiple_of(slab * tn, HDIM), wa_ref, stage_ref, sem_ref)
        _load_weight(w_hbm, pl.multiple_of(d_ff + slab * tn, HDIM), wg_ref, stage_ref, sem_ref)

    def conv(u, t_ref, cw_ref, cb_ref, cols, scale):
        cw = cw_ref[:, cols] * scale
        cb = cb_ref[:, cols] * scale
        last = t_ref[2 * SUBLANES - 1:2 * SUBLANES, cols]
        last2 = t_ref[SUBLANES - 1:SUBLANES, cols]
        z = _shift1(cw[0:1, :] * u, cw[0:1, :] * last) + cw[1:2, :] * u
        y = _shift1(z, cw[0:1, :] * last2 + cw[1:2, :] * last) + (cw[2:3, :] * u + cb)
        t_ref[:, cols] = u[tm - 2 * SUBLANES:tm]
        return y

    h = h_ref[...]
    for bi, c0 in enumerate(range(0, tn, cblk)):
        cols = slice(c0, c0 + cblk)
        ya = conv(_dot(h, wa_ref[:, cols]), ta_ref, cwa_ref, cba_ref, cols, 1.0)
        yh = conv(_dot(h, wg_ref[:, cols]), tg_ref, cwg_ref, cbg_ref, cols, 0.5)
        act = ya * (yh * (1.0 + jnp.tanh(yh)))
        perm_ref = perm_refs[bi % 2]
        for l in range(cblk // HDIM):
            perm_ref[l] = act[:, l * HDIM:(l + 1) * HDIM]
            lanes = slice(c0 + l * HDIM, c0 + (l + 1) * HDIM)
            for s in range(SUBLANES):
                o_ref[s * PERM_R:(s + 1) * PERM_R, lanes] = (
                    perm_ref[l, pl.ds(s, PERM_R, stride=SUBLANES), :].astype(o_ref.dtype))


def _ffn_up(h2, w_up, conv_w, conv_b, tm, n_slabs, cblk):
    S, D = h2.shape
    d_ff = w_up.shape[1] // 2
    tn = d_ff // n_slabs
    a_col = lambda r: pl.BlockSpec((r, tn), lambda c, m: (0, c))
    g_col = lambda r: pl.BlockSpec((r, tn), lambda c, m: (0, n_slabs + c))
    return pl.pallas_call(
        functools.partial(_ffn_up_kernel, cblk=cblk),
        grid=(n_slabs, S // tm),
        in_specs=[pl.BlockSpec((tm, D), lambda c, m: (m, 0)),
                  pl.BlockSpec(memory_space=pl.ANY),
                  a_col(CONV_W), g_col(CONV_W), a_col(1), g_col(1)],
        out_specs=pl.BlockSpec((tm, tn), lambda c, m: (m, c)),
        out_shape=jax.ShapeDtypeStruct((S, d_ff), BF16),
        scratch_shapes=([pltpu.VMEM((2 * SUBLANES, tn), F32)] * 2
                        + [pltpu.VMEM((D, tn), BF16)] * 2 + _weight_scratch(D, tn)[1:]
                        + [pltpu.VMEM((cblk // HDIM, tm, HDIM), F32)] * 2),
        compiler_params=pltpu.CompilerParams(
            dimension_semantics=("arbitrary", "arbitrary"), vmem_limit_bytes=VMEM_LIMIT_BIG),
        name="ffn_up",
    )(h2, w_up, conv_w, conv_w, conv_b, conv_b)


def _ffn_down_kernel(a_ref, w_hbm, x1_ref, o_ref, w_ref, stage_ref, sem_ref):
    @pl.when(pl.program_id(0) == 0)
    def _():
        _load_weight(w_hbm, 0, w_ref, stage_ref, sem_ref)

    o_ref[...] = x1_ref[...] + _dot(a_ref[...], w_ref[...])


def _ffn_down(act, w_down, x1, tm):
    S, D = x1.shape
    d_ff = act.shape[1]
    return pl.pallas_call(
        _ffn_down_kernel,
        grid=(S // tm,),
        in_specs=[pl.BlockSpec((tm, d_ff), lambda i: (i, 0)),
                  pl.BlockSpec(memory_space=pl.ANY),
                  pl.BlockSpec((tm, D), lambda i: (i, 0))],
        out_specs=pl.BlockSpec((tm, D), lambda i: (i, 0)),
        out_shape=jax.ShapeDtypeStruct((S, D), F32),
        scratch_shapes=_weight_scratch(d_ff, D),
        compiler_params=pltpu.CompilerParams(
            dimension_semantics=("arbitrary",), vmem_limit_bytes=VMEM_LIMIT),
        name="ffn_down",
    )(act, w_down, x1)


def kernel(x, g_mix, w_in, hg_lb, hg_out_gain, q_gain, k_gain, rel_bias, at_out_gain, w_out,
           g_ffn, w_up, conv_w, conv_b, w_down):
    B, S, D = x.shape
    depth = g_mix.shape[0]
    assert B == 1 and depth == 1 and S % 1024 == 0
    assert w_in.shape[2] == N_GROUPS * GROUP_W
    x2 = x.reshape(S, D)
    proj, b = _inproj(x2, g_mix, w_in[0], hg_lb, q_gain, k_gain, tm=HG_BLOCK)
    o_hg = _hgrn(proj, b, hg_out_gain)
    o_at = _attention(proj, _rel_bias(rel_bias[0]))
    x1, h2 = _outproj(o_hg, o_at, x2, w_out[0], at_out_gain, g_ffn, tm=512)
    act = _ffn_up(h2, w_up[0], conv_w[0], conv_b, tm=256, n_slabs=1, cblk=256)
    out = _ffn_down(act, w_down[0], x1, tm=512)
    return out.reshape(B, S, D)
```
